```python
import math, functools
import jax, jax.numpy as jnp
from jax import lax
import numpy as np

D_MODEL = 1024
BATCH = 32
SEQ = 2048
DEPTH = 1
DEC_BATCH = 128
DEC_SEQ = 4
PAST_LEN = 16384
PAGE_SIZE = 128

N_META = 16
D_MIX = D_MODEL
D_CONV = D_MIX // 2
N_CONV_GROUPS = 8
CONV_WIDTH = 31
N_HEADS = 8
QK_NOPE = 64
QK_ROPE = 32
QK_DIM = QK_NOPE + QK_ROPE
V_DIM = 64
D_ATTN = N_HEADS * V_DIM
Q_LORA = 256
KV_LORA = 128
ROPE_THETA = 10000.0
D_FF = 2816
FFN_CONV_WIDTH = 3
Q_BLOCK = 128
EPS = 1e-6
D_IN = 2 * D_CONV + Q_LORA + KV_LORA + QK_ROPE

kernel_name = "hymba_mla_conformer_convffn_step"


def rms_norm(x, g):
    xf = x.astype(jnp.float32)
    y = xf * lax.rsqrt(jnp.mean(xf * xf, axis=-1, keepdims=True) + EPS)
    return (y * g.astype(jnp.float32)).astype(x.dtype)


def layer_norm(x, g, b):
    xf = x.astype(jnp.float32)
    mu = jnp.mean(xf, axis=-1, keepdims=True)
    var = jnp.mean(jnp.square(xf - mu), axis=-1, keepdims=True)
    y = (xf - mu) * lax.rsqrt(var + EPS)
    return (y * g.astype(jnp.float32) + b.astype(jnp.float32)).astype(x.dtype)


def rope_tables(pos):
    inv_freq = ROPE_THETA ** (-jnp.arange(0, QK_ROPE, 2, dtype=jnp.float32) / QK_ROPE)
    ang = pos.astype(jnp.float32)[:, None] * inv_freq[None, :]
    return jnp.cos(ang), jnp.sin(ang)


def apply_rope(x, cos, sin):
    half = QK_ROPE // 2
    x1 = x[..., :half].astype(jnp.float32)
    x2 = x[..., half:].astype(jnp.float32)
    out = jnp.concatenate([x1 * cos - x2 * sin, x2 * cos + x1 * sin], axis=-1)
    return out.astype(x.dtype)


def causal_dwconv(x_ext, w, b):
    c = x_ext.shape[-1]
    y = lax.conv_general_dilated(x_ext, w[:, None, :].astype(x_ext.dtype), window_strides=(1,),
                                 padding='VALID', dimension_numbers=('NWC', 'WIO', 'NWC'),
                                 feature_group_count=c)
    return y + b.astype(x_ext.dtype)


def mixer_projections(h, pos, w_in, q_a_norm_g, w_uq, kv_a_norm_g, q_norm_g):
    b, t, _ = h.shape
    proj = h @ w_in
    a, gate, c_q, c_kv, k_pe = jnp.split(
        proj, [D_CONV, 2 * D_CONV, 2 * D_CONV + Q_LORA, 2 * D_CONV + Q_LORA + KV_LORA], axis=-1)
    glu = a * jax.nn.sigmoid(gate)
    cos, sin = rope_tables(pos)
    q = (rms_norm(c_q, q_a_norm_g) @ w_uq).reshape(b, t, N_HEADS, QK_DIM)
    q = jnp.concatenate([q[..., :QK_NOPE],
                         apply_rope(q[..., QK_NOPE:], cos[:, None, :], sin[:, None, :])], axis=-1)
    q = rms_norm(q, q_norm_g)
    ckv = rms_norm(c_kv, kv_a_norm_g)
    kpe = apply_rope(k_pe, cos, sin)
    return glu, q, ckv, kpe


def expand_keys_values(ckv, kpe, w_ukv, k_norm_g):
    kv = jnp.einsum('btc,chd->bthd', ckv, w_ukv.reshape(KV_LORA, N_HEADS, QK_NOPE + V_DIM))
    k_nope, v = kv[..., :QK_NOPE], kv[..., QK_NOPE:]
    k_rope = jnp.broadcast_to(kpe[:, :, None, :], k_nope.shape[:-1] + (QK_ROPE,))
    k = rms_norm(jnp.concatenate([k_nope, k_rope], axis=-1), k_norm_g)
    return k, v


def prompt_attention(q, ckv, kpe, w_ukv, k_norm_g):
    k, v = expand_keys_values(ckv, kpe, w_ukv, k_norm_g)
    b, length = q.shape[:2]
    n_blk = -(-length // Q_BLOCK)
    lp = n_blk * Q_BLOCK
    qp = jnp.pad(q, ((0, 0), (0, lp - length), (0, 0), (0, 0)))
    qb = qp.reshape(b, n_blk, Q_BLOCK, N_HEADS, QK_DIM).transpose(1, 0, 2, 3, 4)
    kpos = jnp.arange(length)
    scale = QK_DIM ** -0.5

    def block(args):
        qi, i = args
        s = jnp.einsum('bqhd,bkhd->bhqk', qi, k).astype(jnp.float32) * scale
        qpos = i * Q_BLOCK + jnp.arange(Q_BLOCK)
        s = jnp.where(kpos[None, :] <= qpos[:, None], s, -jnp.inf)
        p = jax.nn.softmax(s, axis=-1).astype(v.dtype)
        return jnp.einsum('bhqk,bkhd->bqhd', p, v)

    o = lax.map(block, (qb, jnp.arange(n_blk)))
    return o.transpose(1, 0, 2, 3, 4).reshape(b, lp, N_HEADS, V_DIM)[:, :length]


def sample_attention(q, ckv, kpe, cache_ckv, cache_kpe, page_table, w_ukv, k_norm_g):
    b, s_len = q.shape[:2]
    scale = QK_DIM ** -0.5

    def page_step(carry, phys):
        m, l, acc = carry
        k, v = expand_keys_values(cache_ckv[phys], cache_kpe[phys], w_ukv, k_norm_g)
        s = jnp.einsum('bqhd,bkhd->bhqk', q, k).astype(jnp.float32) * scale
        m_new = jnp.maximum(m, jnp.max(s, axis=-1))
        corr = jnp.exp(m - m_new)
        p = jnp.exp(s - m_new[..., None])
        l = l * corr + jnp.sum(p, axis=-1)
        acc = acc * corr[..., None] + jnp.einsum('bhqk,bkhd->bhqd', p, v.astype(jnp.float32))
        return (m_new, l, acc), None

    init = (jnp.full((b, N_HEADS, s_len), -jnp.inf, jnp.float32),
            jnp.zeros((b, N_HEADS, s_len), jnp.float32),
            jnp.zeros((b, N_HEADS, s_len, V_DIM), jnp.float32))
    (m, l, acc), _ = lax.scan(page_step, init, page_table.T)
    k_new, v_new = expand_keys_values(ckv, kpe, w_ukv, k_norm_g)
    s = jnp.einsum('bqhd,bkhd->bhqk', q, k_new).astype(jnp.float32) * scale
    s = jnp.where(jnp.tril(jnp.ones((s_len, s_len), dtype=bool)), s, -jnp.inf)
    m_fin = jnp.maximum(m, jnp.max(s, axis=-1))
    corr = jnp.exp(m - m_fin)
    p = jnp.exp(s - m_fin[..., None])
    l = l * corr + jnp.sum(p, axis=-1)
    acc = acc * corr[..., None] + jnp.einsum('bhqk,bkhd->bhqd', p, v_new.astype(jnp.float32))
    o = (acc / l[..., None]).astype(q.dtype)
    return o.transpose(0, 2, 1, 3)


def decoder_layer(x, pos, conv_hist, ffn_hist, attend, norm_mix_g, w_in, q_a_norm_g, w_uq,
                  kv_a_norm_g, q_norm_g, conv_dw_w, conv_dw_b, conv_ln_g, conv_ln_b,
                  out_norm_attn_g, out_norm_conv_g, w_out, norm_ffn_g, w_ffn_up, ffn_dw_w,
                  ffn_dw_b, w_ffn_down):
    b, t, _ = x.shape
    h = rms_norm(x, norm_mix_g)
    glu, q, ckv, kpe = mixer_projections(h, pos, w_in, q_a_norm_g, w_uq, kv_a_norm_g, q_norm_g)
    attn_o = attend(q, ckv, kpe).reshape(b, t, D_ATTN)
    conv_ext = jnp.concatenate([conv_hist.astype(glu.dtype), glu], axis=1)
    conv_y = jax.nn.silu(layer_norm(causal_dwconv(conv_ext, conv_dw_w, conv_dw_b), conv_ln_g, conv_ln_b))
    merged = jnp.concatenate([rms_norm(attn_o, out_norm_attn_g), rms_norm(conv_y, out_norm_conv_g)], axis=-1)
    x = x + merged @ w_out
    u = rms_norm(x, norm_ffn_g) @ w_ffn_up
    ffn_ext = jnp.concatenate([ffn_hist.astype(u.dtype), u], axis=1)
    uc = causal_dwconv(ffn_ext, ffn_dw_w, ffn_dw_b)
    g, up = uc[..., :D_FF], uc[..., D_FF:]
    x = x + (jax.nn.silu(g) * up) @ w_ffn_down
    return x, ckv, kpe, conv_ext[:, -(CONV_WIDTH - 1):], ffn_ext[:, -(FFN_CONV_WIDTH - 1):]


def setup_inputs(seed: int = 0) -> dict:
    key = jax.random.key(seed)
    ks = jax.random.split(key, 32)
    n_pages = PAST_LEN // PAGE_SIZE
    n_used = DEC_BATCH * n_pages
    n_pool = (n_used * 5) // 4
    f32 = jnp.float32

    def nrm(k, shape, scale):
        return jax.random.normal(k, shape, f32) * scale

    def gain(k, shape):
        return 1.0 + 0.01 * jax.random.normal(k, shape, f32)

    page_table = jax.random.permutation(ks[6], n_pool)[:n_used].reshape(DEC_BATCH, n_pages).astype(jnp.int32)
    return {
        "x_prompt": nrm(ks[0], (BATCH, SEQ, D_MODEL), 1.0),
        "x_sample": nrm(ks[1], (DEC_BATCH, DEC_SEQ, D_MODEL), 1.0),
        "cache_ckv": nrm(ks[2], (DEPTH, n_pool, PAGE_SIZE, KV_LORA), 1.0),
        "cache_kpe": nrm(ks[3], (DEPTH, n_pool, PAGE_SIZE, QK_ROPE), 1.0),
        "state_conv": nrm(ks[4], (DEPTH, DEC_BATCH, CONV_WIDTH - 1, D_CONV), 0.5),
        "state_ffn": nrm(ks[5], (DEPTH, DEC_BATCH, FFN_CONV_WIDTH - 1, 2 * D_FF), 1.0),
        "page_table": page_table,
        "meta_tokens": nrm(ks[7], (N_META, D_MODEL), 1.0),
        "norm_mix_g": gain(ks[8], (DEPTH, D_MODEL)),
        "w_in": nrm(ks[9], (DEPTH, D_MODEL, D_IN), D_MODEL ** -0.5),
        "q_a_norm_g": gain(ks[10], (DEPTH, Q_LORA)),
        "w_uq": nrm(ks[11], (DEPTH, Q_LORA, N_HEADS * QK_DIM), Q_LORA ** -0.5),
        "kv_a_norm_g": gain(ks[12], (DEPTH, KV_LORA)),
        "w_ukv": nrm(ks[13], (DEPTH, KV_LORA, N_HEADS * (QK_NOPE + V_DIM)), KV_LORA ** -0.5),
        "q_norm_g": gain(ks[14], (DEPTH, QK_DIM)),
        "k_norm_g": gain(ks[15], (DEPTH, QK_DIM)),
        "conv_dw_w": nrm(ks[16], (DEPTH, CONV_WIDTH, D_CONV), CONV_WIDTH ** -0.5),
        "conv_dw_b": nrm(ks[17], (DEPTH, D_CONV), 0.02),
        "conv_ln_g": gain(ks[18], (DEPTH, D_CONV)),
        "conv_ln_b": nrm(ks[19], (DEPTH, D_CONV), 0.02),
        "out_norm_attn_g": gain(ks[20], (DEPTH, D_ATTN)),
        "out_norm_conv_g": gain(ks[21], (DEPTH, D_CONV)),
        "w_out": nrm(ks[22], (DEPTH, D_MIX, D_MODEL), D_MIX ** -0.5),
        "norm_ffn_g": gain(ks[23], (DEPTH, D_MODEL)),
        "w_ffn_up": nrm(ks[24], (DEPTH, D_MODEL, 2 * D_FF), D_MODEL ** -0.5),
        "ffn_dw_w": nrm(ks[25], (DEPTH, FFN_CONV_WIDTH, 2 * D_FF), FFN_CONV_WIDTH ** -0.5),
        "ffn_dw_b": nrm(ks[26], (DEPTH, 2 * D_FF), 0.02),
        "w_ffn_down": nrm(ks[27], (DEPTH, D_FF, D_MODEL), D_FF ** -0.5),
    }


def reference(x_prompt, x_sample, cache_ckv, cache_kpe, state_conv, state_ffn, page_table,
              meta_tokens, norm_mix_g, w_in, q_a_norm_g, w_uq, kv_a_norm_g, w_ukv, q_norm_g,
              k_norm_g, conv_dw_w, conv_dw_b, conv_ln_g, conv_ln_b, out_norm_attn_g,
              out_norm_conv_g, w_out, norm_ffn_g, w_ffn_up, ffn_dw_w, ffn_dw_b, w_ffn_down):
    b_p = x_prompt.shape[0]
    meta = jnp.broadcast_to(meta_tokens[None].astype(x_prompt.dtype), (b_p, N_META, D_MODEL))
    xp = jnp.concatenate([meta, x_prompt], axis=1)
    pos_p = jnp.arange(xp.shape[1], dtype=jnp.int32)
    xs = x_sample
    past_len = page_table.shape[1] * cache_ckv.shape[2]
    pos_s = past_len + jnp.arange(xs.shape[1], dtype=jnp.int32)
    conv_hist_p = jnp.zeros((b_p, CONV_WIDTH - 1, D_CONV), xp.dtype)
    ffn_hist_p = jnp.zeros((b_p, FFN_CONV_WIDTH - 1, 2 * D_FF), xp.dtype)

    ckv_p, kpe_p, conv_p, ffn_p = [], [], [], []
    ckv_s, kpe_s, conv_s, ffn_s = [], [], [], []
    for layer in range(DEPTH):
        weights = (norm_mix_g[layer], w_in[layer], q_a_norm_g[layer], w_uq[layer],
                   kv_a_norm_g[layer], q_norm_g[layer], conv_dw_w[layer], conv_dw_b[layer],
                   conv_ln_g[layer], conv_ln_b[layer], out_norm_attn_g[layer],
                   out_norm_conv_g[layer], w_out[layer], norm_ffn_g[layer], w_ffn_up[layer],
                   ffn_dw_w[layer], ffn_dw_b[layer], w_ffn_down[layer])
        attend_p = functools.partial(prompt_attention, w_ukv=w_ukv[layer], k_norm_g=k_norm_g[layer])
        attend_s = functools.partial(sample_attention, cache_ckv=cache_ckv[layer],
                                     cache_kpe=cache_kpe[layer], page_table=page_table,
                                     w_ukv=w_ukv[layer], k_norm_g=k_norm_g[layer])
        xp, c1, k1, cv1, f1 = decoder_layer(xp, pos_p, conv_hist_p, ffn_hist_p, attend_p, *weights)
        xs, c2, k2, cv2, f2 = decoder_layer(xs, pos_s, state_conv[layer], state_ffn[layer], attend_s, *weights)
        ckv_p.append(c1); kpe_p.append(k1); conv_p.append(cv1); ffn_p.append(f1)
        ckv_s.append(c2); kpe_s.append(k2); conv_s.append(cv2); ffn_s.append(f2)

    y_prompt = xp[:, N_META:]
    y_sample = xs
    return (y_prompt, y_sample,
            jnp.stack(ckv_p), jnp.stack(kpe_p), jnp.stack(conv_p), jnp.stack(ffn_p),
            jnp.stack(ckv_s), jnp.stack(kpe_s), jnp.stack(conv_s), jnp.stack(ffn_s))
```

```python
import functools

import jax
import jax.numpy as jnp
from jax import lax
from jax.experimental import pallas as pl
from jax.experimental.pallas import tpu as pltpu

F32 = jnp.float32
BF16 = jnp.bfloat16

D_MODEL = 1024
N_META = 16
D_CONV = 512
CONV_WIDTH = 31
N_HEADS = 8
QK_NOPE = 64
QK_ROPE = 32
QK_DIM = QK_NOPE + QK_ROPE
V_DIM = 64
D_ATTN = N_HEADS * V_DIM
Q_LORA = 256
KV_LORA = 128
ROPE_THETA = 10000.0
D_FF = 2816
FFN_CONV_WIDTH = 3
EPS = 1e-6

LANES = 128
SUBLANES = 8
HEAD_LANES = LANES
D_QK_PAD = N_HEADS * HEAD_LANES
D_IN_PAD = 2 * D_CONV + Q_LORA + KV_LORA + LANES
CONV_HIST_PAD = 32
FFN_HIST_PAD = 16
FFN_CHUNK = 256
VMEM_LIMIT = 56 * 1024 * 1024


def _cparams(sem):
    return pltpu.CompilerParams(dimension_semantics=sem, vmem_limit_bytes=VMEM_LIMIT)


def _rms_scale(x, n):
    return lax.rsqrt(jnp.sum(x * x, axis=-1, keepdims=True) * (1.0 / n) + EPS)


def _const_spec(shape):
    nd = len(shape)
    return pl.BlockSpec(shape, lambda *_: (0,) * nd)


def _proj_kernel(x_ref, cos_ref, sin_ref, gmix_ref, win_ref, gqa_ref, wuq_ref, gkva_ref, wuk_ref,
                 wuvT_ref, gq_ref, gk_ref, glu_ref, q_ref, k_ref, vT_ref, ckv_ref, kpe_ref, *, tq):
    x = x_ref[0]
    t = x.shape[0]
    h = (x * _rms_scale(x, D_MODEL) * gmix_ref[...]).astype(BF16)
    proj = jnp.dot(h, win_ref[...], preferred_element_type=F32)
    glu_ref[0] = proj[:, :D_CONV] * jax.nn.sigmoid(proj[:, D_CONV:2 * D_CONV])
    o = 2 * D_CONV
    cq = proj[:, o:o + Q_LORA]
    ckv_raw = proj[:, o + Q_LORA:o + Q_LORA + KV_LORA]
    kp = proj[:, o + Q_LORA + KV_LORA:]

    cqn = (cq * _rms_scale(cq, Q_LORA) * gqa_ref[...]).astype(BF16)
    qq = jnp.dot(cqn, wuq_ref[...], preferred_element_type=F32)
    ckv = ckv_raw * _rms_scale(ckv_raw, KV_LORA) * gkva_ref[...]
    ckv_ref[0] = ckv
    cos = cos_ref[...]
    sin = sin_ref[...]
    kpe = kp * cos + pltpu.roll(kp, LANES - QK_ROPE, 1) * sin
    kpe_ref[0] = kpe[:, :QK_ROPE]
    ckv_b = ckv.astype(BF16)
    kn = jnp.dot(ckv_b, wuk_ref[...], preferred_element_type=F32)
    vT = lax.dot_general(wuvT_ref[...], ckv_b, (((1,), (1,)), ((), ())),
                         preferred_element_type=F32).astype(BF16)
    for j in range(t // tq):
        vT_ref[0, j] = vT[:, j * tq:(j + 1) * tq]

    lane = lax.broadcasted_iota(jnp.int32, (1, LANES), 1)
    nope = jnp.where((lane >= QK_ROPE) & (lane < QK_DIM), 1.0, 0.0).astype(F32)
    cosq = cos + nope
    gq = gq_ref[...]
    gk = gk_ref[...]
    for hd in range(N_HEADS):
        sl = slice(hd * HEAD_LANES, (hd + 1) * HEAD_LANES)
        qh = qq[:, sl] * cosq + qq[:, D_QK_PAD + hd * HEAD_LANES:D_QK_PAD + (hd + 1) * HEAD_LANES] * sin
        q_ref[0, :, sl] = (qh * _rms_scale(qh, QK_DIM) * gq).astype(BF16)
        kh = kn[:, sl] + kpe
        k_ref[0, :, sl] = (kh * _rms_scale(kh, QK_DIM) * gk).astype(BF16)


def _proj(x, cos, sin, w, *, t, tq):
    b, l, _ = x.shape
    kern = functools.partial(_proj_kernel, tq=tq)
    out_shape = (
        jax.ShapeDtypeStruct((b, l, D_CONV), F32),
        jax.ShapeDtypeStruct((b, l, D_QK_PAD), BF16),
        jax.ShapeDtypeStruct((b, l, D_QK_PAD), BF16),
        jax.ShapeDtypeStruct((b, l // tq, D_ATTN, tq), BF16),
        jax.ShapeDtypeStruct((b, l, KV_LORA), F32),
        jax.ShapeDtypeStruct((b, l, QK_ROPE), F32),
    )
    row = lambda bi, i: (bi, i, 0)
    in_specs = [
        pl.BlockSpec((1, t, D_MODEL), row),
        pl.BlockSpec((t, LANES), lambda bi, i: (i, 0)),
        pl.BlockSpec((t, LANES), lambda bi, i: (i, 0)),
        _const_spec((1, D_MODEL)),
        _const_spec((D_MODEL, D_IN_PAD)),
        _const_spec((1, Q_LORA)),
        _const_spec((Q_LORA, 2 * D_QK_PAD)),
        _const_spec((1, KV_LORA)),
        _const_spec((KV_LORA, D_QK_PAD)),
        _const_spec((D_ATTN, KV_LORA)),
        _const_spec((1, LANES)),
        _const_spec((1, LANES)),
    ]
    out_specs = (
        pl.BlockSpec((1, t, D_CONV), row),
        pl.BlockSpec((1, t, D_QK_PAD), row),
        pl.BlockSpec((1, t, D_QK_PAD), row),
        pl.BlockSpec((1, t // tq, D_ATTN, tq), lambda bi, i: (bi, i, 0, 0)),
        pl.BlockSpec((1, t, KV_LORA), row),
        pl.BlockSpec((1, t, QK_ROPE), row),
    )
    return pl.pallas_call(
        kern, name=f"proj_b{b}", grid=(b, l // t), in_specs=in_specs, out_specs=out_specs,
        out_shape=out_shape,
        compiler_params=_cparams(("parallel", "parallel")),
    )(x, cos, sin, w["gmix"], w["win"], w["gqa"], w["wuq"], w["gkva"], w["wuk"], w["wuvT"],
      w["gq"], w["gk"])


def _softmax_update(m, l, acc, s, vT):
    m_new = jnp.maximum(m, jnp.max(s, axis=0, keepdims=True))
    corr = jnp.exp(m - m_new)
    p = jnp.exp(s - m_new)
    l = l * corr + jnp.sum(p, axis=0, keepdims=True)
    acc = acc * corr + jnp.dot(vT, p.astype(BF16), preferred_element_type=F32)
    return m_new, l, acc


def _attn_kernel(*refs, tq, nq, n_prefix):
    if n_prefix:
        q_ref, k_ref, vT_ref, kp_ref, vTp_ref, o_ref, oT_ref = refs
    else:
        q_ref, k_ref, vT_ref, o_ref, oT_ref = refs
    tb = (((1,), (1,)), ((), ()))
    row_i = lax.broadcasted_iota(jnp.int32, (tq, tq), 0)
    col_i = lax.broadcasted_iota(jnp.int32, (tq, tq), 1)
    causal = row_i <= col_i
    for hh in range(2):
        lanes = slice(hh * HEAD_LANES, (hh + 1) * HEAD_LANES)
        rows = slice(hh * V_DIM, (hh + 1) * V_DIM)

        def q_body(qi, carry, lanes=lanes, rows=rows):
            q0 = pl.multiple_of(qi * tq, tq)
            q_t = q_ref[0, pl.ds(q0, tq), lanes]
            if n_prefix:
                s0 = lax.dot_general(kp_ref[0, :, lanes], q_t, tb, preferred_element_type=F32)
                m = jnp.max(s0, axis=0, keepdims=True)
                p0 = jnp.exp(s0 - m)
                l = jnp.sum(p0, axis=0, keepdims=True)
                acc = jnp.dot(vTp_ref[0, rows, :], p0.astype(BF16), preferred_element_type=F32)
            else:
                m = jnp.full((1, tq), -jnp.inf, F32)
                l = jnp.zeros((1, tq), F32)
                acc = jnp.zeros((V_DIM, tq), F32)

            def k_body(kj, c):
                k0 = pl.multiple_of(kj * tq, tq)
                s = lax.dot_general(k_ref[0, pl.ds(k0, tq), lanes], q_t, tb,
                                    preferred_element_type=F32)
                return _softmax_update(*c, s, vT_ref[0, kj, rows, :])

            m, l, acc = lax.fori_loop(0, qi, k_body, (m, l, acc))
            s = lax.dot_general(k_ref[0, pl.ds(q0, tq), lanes], q_t, tb, preferred_element_type=F32)
            s = jnp.where(causal, s, -jnp.inf)
            m, l, acc = _softmax_update(m, l, acc, s, vT_ref[0, qi, rows, :])
            oT_ref[qi, rows, :] = acc / l
            return carry

        lax.fori_loop(0, nq, q_body, 0)
    for qi in range(nq):
        o_ref[0, qi * tq:(qi + 1) * tq, :] = oT_ref[qi].T


def _attention(q, k, vT, k_prefix=None, vT_prefix=None, *, tq):
    b, l, _ = q.shape
    nq = l // tq
    n_prefix = 0 if k_prefix is None else k_prefix.shape[1]
    kern = functools.partial(_attn_kernel, tq=tq, nq=nq, n_prefix=n_prefix)
    pair = 2 * HEAD_LANES
    in_specs = [
        pl.BlockSpec((1, l, pair), lambda bi, g: (bi, 0, g)),
        pl.BlockSpec((1, l, pair), lambda bi, g: (bi, 0, g)),
        pl.BlockSpec((1, nq, 2 * V_DIM, tq), lambda bi, g: (bi, 0, g, 0)),
    ]
    args = [q, k, vT]
    if n_prefix:
        in_specs += [
            pl.BlockSpec((1, n_prefix, pair), lambda bi, g: (0, 0, g)),
            pl.BlockSpec((1, 2 * V_DIM, n_prefix), lambda bi, g: (0, g, 0)),
        ]
        args += [k_prefix, vT_prefix]
    return pl.pallas_call(
        kern, name=f"attn_b{b}", grid=(b, N_HEADS // 2), in_specs=in_specs,
        out_specs=pl.BlockSpec((1, l, 2 * V_DIM), lambda bi, g: (bi, 0, g)),
        out_shape=jax.ShapeDtypeStruct((b, l, D_ATTN), F32),
        scratch_shapes=[pltpu.VMEM((nq, 2 * V_DIM, tq), F32)],
        compiler_params=_cparams(("parallel", "parallel")),
    )(*args)


def _conv_tail(y, lng, lnb, gconv):
    mu = jnp.mean(y, axis=-1, keepdims=True)
    yc = y - mu
    var = jnp.mean(yc * yc, axis=-1, keepdims=True)
    z = yc * lax.rsqrt(var + EPS) * lng + lnb
    z = z * jax.nn.sigmoid(z)
    return (z * _rms_scale(z, D_CONV) * gconv).astype(BF16)


def _convout_kernel(x_ref, glu_ref, prev_ref, hist_ref, ao_ref, cw_ref, cb_ref, lng_ref, lnb_ref,
                    gattn_ref, gconv_ref, wout_ref, o_ref, ext_ref, y_ref, *, rc):
    i = pl.program_id(1)
    t = glu_ref.shape[1]
    ext_ref[0:CONV_HIST_PAD] = jnp.where(i == 0, hist_ref[0], prev_ref[0])
    ext_ref[CONV_HIST_PAD:] = glu_ref[0]
    base = CONV_HIST_PAD - (CONV_WIDTH - 1)
    for cc in range(D_CONV // LANES):
        cs = slice(cc * LANES, (cc + 1) * LANES)
        wcol = cw_ref[:, cs]
        for r0 in range(0, t, rc):
            acc = jnp.zeros((rc, LANES), F32)
            for kk in range(CONV_WIDTH):
                acc = acc + wcol[kk:kk + 1] * ext_ref[r0 + base + kk:r0 + base + kk + rc, cs]
            y_ref[r0:r0 + rc, cs] = acc
    conv_n = _conv_tail(y_ref[...] + cb_ref[...], lng_ref[...], lnb_ref[...], gconv_ref[...])
    ao = ao_ref[0]
    attn_n = (ao * _rms_scale(ao, D_ATTN) * gattn_ref[...]).astype(BF16)
    upd = jnp.dot(attn_n, wout_ref[0:D_ATTN], preferred_element_type=F32)
    upd = upd + jnp.dot(conv_n, wout_ref[D_ATTN:], preferred_element_type=F32)
    o_ref[0] = x_ref[0] + upd


def _convout(x, glu, hist, ao, w, *, t):
    b, l, _ = x.shape
    per_batch_hist = hist.shape[0] > 1
    hb = t // CONV_HIST_PAD
    row = lambda bi, i: (bi, i, 0)
    in_specs = [
        pl.BlockSpec((1, t, D_MODEL), row),
        pl.BlockSpec((1, t, D_CONV), row),
        pl.BlockSpec((1, CONV_HIST_PAD, D_CONV), lambda bi, i: (bi, jnp.maximum(i * hb - 1, 0), 0)),
        pl.BlockSpec((1, CONV_HIST_PAD, D_CONV),
                     (lambda bi, i: (bi, 0, 0)) if per_batch_hist else (lambda bi, i: (0, 0, 0))),
        pl.BlockSpec((1, t, D_ATTN), row),
        _const_spec((CONV_HIST_PAD, D_CONV)),
        _const_spec((1, D_CONV)),
        _const_spec((1, D_CONV)),
        _const_spec((1, D_CONV)),
        _const_spec((1, D_ATTN)),
        _const_spec((1, D_CONV)),
        _const_spec((D_MODEL, D_MODEL)),
    ]
    return pl.pallas_call(
        functools.partial(_convout_kernel, rc=min(t, 128)), name=f"convout_b{b}",
        grid=(b, l // t), in_specs=in_specs,
        out_specs=pl.BlockSpec((1, t, D_MODEL), row),
        out_shape=jax.ShapeDtypeStruct((b, l, D_MODEL), F32),
        scratch_shapes=[pltpu.VMEM((t + CONV_HIST_PAD, D_CONV), F32), pltpu.VMEM((t, D_CONV), F32)],
        compiler_params=_cparams(("parallel", "arbitrary")),
    )(x, glu, glu, hist, ao, w["conv_w"], w["conv_b"], w["ln_g"], w["ln_b"], w["gattn"],
      w["gconv"], w["wout"])


def _ffn_kernel(*refs, ts, has_prev, st0):
    if has_prev:
        x_ref, prev_ref, hist_ref, g_ref, wup_ref, cw_ref, cb_ref, wdn_ref, o_ref, st_ref, act_ref = refs
    else:
        x_ref, hist_ref, g_ref, wup_ref, cw_ref, cb_ref, wdn_ref, o_ref, st_ref, act_ref = refs
    i = pl.program_id(1)
    x = x_ref[0]
    t = x.shape[0]
    hr = hist_ref.shape[1]
    g = g_ref[...]
    if has_prev:
        xe = jnp.concatenate([prev_ref[0], x], axis=0)
    else:
        xe = x
    hx = (xe * _rms_scale(xe, D_MODEL) * g).astype(BF16)
    cw = cw_ref[...]
    cb = cb_ref[...]
    w2 = 2 * FFN_CHUNK
    for c in range(D_FF // FFN_CHUNK):
        cs = slice(c * w2, (c + 1) * w2)
        u = jnp.dot(hx, wup_ref[:, cs], preferred_element_type=F32)
        hist = hist_ref[0, :, cs]
        if has_prev:
            u_ext = jnp.concatenate([jnp.where(i == 0, hist, u[:hr]), u[hr:]], axis=0)
        else:
            u_ext = jnp.concatenate([hist, u], axis=0)
        st_ref[0, :, cs] = u_ext[st0:st0 + 2 * ts]
        uc = (cw[0:1, cs] * u_ext[hr - 2 * ts:hr - 2 * ts + t]
              + cw[1:2, cs] * u_ext[hr - ts:hr - ts + t]
              + cw[2:3, cs] * u_ext[hr:] + cb[:, cs])
        gate = uc[:, :FFN_CHUNK]
        act = gate * jax.nn.sigmoid(gate) * uc[:, FFN_CHUNK:]
        act_ref[:, c * FFN_CHUNK:(c + 1) * FFN_CHUNK] = act.astype(BF16)
    o_ref[0] = x + jnp.dot(act_ref[...], wdn_ref[...], preferred_element_type=F32)


def _ffn(x, hist, w, *, t, ts, has_prev, n_valid=None):
    b, l, _ = x.shape
    hr = hist.shape[1]
    st0 = hr + (t if n_valid is None else n_valid) - 2 * ts
    per_batch_hist = hist.shape[0] > 1
    row = lambda bi, i: (bi, i, 0)
    in_specs = [pl.BlockSpec((1, t, D_MODEL), row)]
    args = [x]
    if has_prev:
        hb = t // hr
        in_specs.append(pl.BlockSpec((1, hr, D_MODEL), lambda bi, i: (bi, jnp.maximum(i * hb - 1, 0), 0)))
        args.append(x)
    in_specs += [
        pl.BlockSpec((1, hr, 2 * D_FF),
                     (lambda bi, i: (bi, 0, 0)) if per_batch_hist else (lambda bi, i: (0, 0, 0))),
        _const_spec((1, D_MODEL)),
        _const_spec((D_MODEL, 2 * D_FF)),
        _const_spec((SUBLANES, 2 * D_FF)),
        _const_spec((1, 2 * D_FF)),
        _const_spec((D_FF, D_MODEL)),
    ]
    args += [hist, w["gffn"], w["wup"], w["ffn_w"], w["ffn_b"], w["wdn"]]
    out, st = pl.pallas_call(
        functools.partial(_ffn_kernel, ts=ts, has_prev=has_prev, st0=st0), name=f"ffn_b{b}_ts{ts}",
        grid=(b, l // t), in_specs=in_specs,
        out_specs=(pl.BlockSpec((1, t, D_MODEL), row),
                   pl.BlockSpec((1, 2 * ts, 2 * D_FF), lambda bi, i: (bi, 0, 0))),
        out_shape=(jax.ShapeDtypeStruct((b, l, D_MODEL), F32),
                   jax.ShapeDtypeStruct((b, 2 * ts, 2 * D_FF), F32)),
        scratch_shapes=[pltpu.VMEM((t, D_FF), BF16)],
        compiler_params=_cparams(("parallel", "arbitrary")),
    )(*args)
    return out, st


def _qlat_kernel(q_ref, gk_ref, wuk_ref, o_ref):
    gk = gk_ref[...]
    n = q_ref.shape[0]
    for hd in range(N_HEADS):
        sl = slice(hd * HEAD_LANES, (hd + 1) * HEAD_LANES)
        qg = q_ref[:, sl].astype(F32) * gk
        lat = lax.dot_general(qg.astype(BF16), wuk_ref[:, sl], (((1,), (1,)), ((), ())),
                              preferred_element_type=F32)
        lane = lax.broadcasted_iota(jnp.int32, (1, LANES), 1)
        rope = jnp.where(lane < QK_ROPE, qg, 0.0)
        o_ref[hd] = jnp.concatenate([lat, rope], axis=1).astype(BF16)


def _qlat(q, w):
    n = q.shape[0]
    return pl.pallas_call(
        _qlat_kernel, name="sample_qlat",
        out_shape=jax.ShapeDtypeStruct((N_HEADS, n, 2 * LANES), BF16),
    )(q, w["gk"], w["wuk"])


def _sattn_kernel(pt_ref, *refs, pages, page, n_new):
    del pt_ref
    ckv_refs = refs[:pages]
    kpe_refs = refs[pages:2 * pages]
    (newc_ref, newp_ref, aconst_ref, qlat_ref, o_ref,
     a_ref, buf_ref, nbuf_ref, m_ref, l_ref, acc_ref) = refs[2 * pages:]
    c = pl.program_id(1)
    nrow = qlat_ref.shape[1]
    nconst = aconst_ref.shape[0]
    tb = (((1,), (1,)), ((), ()))

    @pl.when(c == 0)
    def _():
        a_ref[0:nconst] = aconst_ref[...]
        a_ref[nconst:] = qlat_ref[0]
        buf_ref[...] = jnp.zeros(buf_ref.shape, BF16)
        nbuf_ref[...] = jnp.zeros(nbuf_ref.shape, BF16)
        m_ref[...] = jnp.full(m_ref.shape, -jnp.inf, F32)
        l_ref[...] = jnp.zeros(l_ref.shape, F32)
        acc_ref[...] = jnp.zeros(acc_ref.shape, F32)

    def scores(tokens_ref):
        r = lax.dot_general(a_ref[...], tokens_ref[...], tb, preferred_element_type=F32)
        tk = r.shape[1]
        kn = r[0:N_HEADS * QK_NOPE]
        ss = jnp.sum((kn * kn).reshape(N_HEADS, QK_NOPE, tk), axis=1)
        pe = r[N_HEADS * QK_NOPE:nconst]
        ss = ss + jnp.sum(pe * pe, axis=0, keepdims=True)
        rinv = lax.rsqrt(ss * (1.0 / QK_DIM) + EPS)
        return r[nconst:] * jnp.concatenate([rinv] * n_new, axis=0)

    def update(s, tokens_ref):
        m = m_ref[...]
        m_new = jnp.maximum(m, jnp.max(s, axis=1, keepdims=True))
        corr = jnp.exp(m - m_new)
        p = jnp.exp(s - m_new)
        l_ref[...] = l_ref[...] * corr + jnp.sum(p, axis=1, keepdims=True)
        acc_ref[...] = acc_ref[...] * corr + jnp.dot(p.astype(BF16), tokens_ref[:, 0:KV_LORA],
                                                     preferred_element_type=F32)
        m_ref[...] = m_new

    for j in range(pages):
        buf_ref[j * page:(j + 1) * page, 0:KV_LORA] = ckv_refs[j][0, 0].astype(BF16)
        buf_ref[j * page:(j + 1) * page, KV_LORA:KV_LORA + QK_ROPE] = kpe_refs[j][0, 0].astype(BF16)
    update(scores(buf_ref), buf_ref)

    @pl.when(c == pl.num_programs(1) - 1)
    def _():
        nbuf_ref[:, 0:KV_LORA] = newc_ref[0].astype(BF16)
        nbuf_ref[:, KV_LORA:KV_LORA + QK_ROPE] = newp_ref[0].astype(BF16)
        s = scores(nbuf_ref)
        tok = lax.broadcasted_iota(jnp.int32, s.shape, 1)
        step = lax.broadcasted_iota(jnp.int32, s.shape, 0) // N_HEADS
        update(jnp.where(tok <= step, s, -jnp.inf), nbuf_ref)
        o_ref[0] = acc_ref[...] / l_ref[...]


def _sample_attention(page_table, cache_ckv, cache_kpe, newc, newp, aconst, qlat, *, pages):
    nb, n_pages = page_table.shape
    page = cache_ckv.shape[2]
    nrow = qlat.shape[1]
    n_new = nrow // N_HEADS
    nconst = aconst.shape[0]

    def ckv_map(j):
        return lambda b, c, pt: (0, pt[b, c * pages + j], 0, 0)

    in_specs = [pl.BlockSpec((1, 1, page, KV_LORA), ckv_map(j)) for j in range(pages)]
    in_specs += [pl.BlockSpec((1, 1, page, QK_ROPE), ckv_map(j)) for j in range(pages)]
    in_specs += [
        pl.BlockSpec((1, LANES, KV_LORA), lambda b, c, pt: (b, 0, 0)),
        pl.BlockSpec((1, LANES, QK_ROPE), lambda b, c, pt: (b, 0, 0)),
        pl.BlockSpec((nconst, 2 * LANES), lambda b, c, pt: (0, 0)),
        pl.BlockSpec((1, nrow, 2 * LANES), lambda b, c, pt: (b, 0, 0)),
    ]
    grid_spec = pltpu.PrefetchScalarGridSpec(
        num_scalar_prefetch=1, grid=(nb, n_pages // pages), in_specs=in_specs,
        out_specs=pl.BlockSpec((1, nrow, KV_LORA), lambda b, c, pt: (b, 0, 0)),
        scratch_shapes=[
            pltpu.VMEM((nconst + nrow, 2 * LANES), BF16),
            pltpu.VMEM((pages * page, 2 * LANES), BF16),
            pltpu.VMEM((LANES, 2 * LANES), BF16),
            pltpu.VMEM((nrow, 1), F32),
            pltpu.VMEM((nrow, 1), F32),
            pltpu.VMEM((nrow, KV_LORA), F32),
        ])
    return pl.pallas_call(
        functools.partial(_sattn_kernel, pages=pages, page=page, n_new=n_new), name="sample_attn",
        grid_spec=grid_spec,
        out_shape=jax.ShapeDtypeStruct((nb, nrow, KV_LORA), F32),
        compiler_params=_cparams(("arbitrary", "arbitrary")),
    )(page_table, *([cache_ckv] * pages), *([cache_kpe] * pages), newc, newp, aconst, qlat)


def _spost_kernel(x_ref, lat_ref, ext_ref, wuvp_ref, cw_ref, cb_ref, lng_ref, lnb_ref, gattn_ref,
                  gconv_ref, wout_ref, o_ref, y_ref, *, nb):
    n = x_ref.shape[0]
    n_new = n // nb
    ao = jnp.zeros((n, D_ATTN), F32)
    for hd in range(N_HEADS):
        ao = ao + jnp.dot(lat_ref[hd].astype(BF16), wuvp_ref[hd], preferred_element_type=F32)
    attn_n = (ao * _rms_scale(ao, D_ATTN) * gattn_ref[...]).astype(BF16)
    cw = cw_ref[...]
    for s in range(n_new):
        acc = jnp.zeros((nb, D_CONV), F32)
        for kk in range(CONV_WIDTH):
            acc = acc + cw[kk:kk + 1] * ext_ref[s + kk]
        y_ref[s * nb:(s + 1) * nb] = acc
    conv_n = _conv_tail(y_ref[...] + cb_ref[...], lng_ref[...], lnb_ref[...], gconv_ref[...])
    upd = jnp.dot(attn_n, wout_ref[0:D_ATTN], preferred_element_type=F32)
    upd = upd + jnp.dot(conv_n, wout_ref[D_ATTN:], preferred_element_type=F32)
    o_ref[...] = x_ref[...] + upd


def _sample_post(x, lat, ext, w, *, nb):
    n = x.shape[0]
    return pl.pallas_call(
        functools.partial(_spost_kernel, nb=nb), name="sample_post",
        out_shape=jax.ShapeDtypeStruct((n, D_MODEL), F32),
        scratch_shapes=[pltpu.VMEM((n, D_CONV), F32)],
        compiler_params=pltpu.CompilerParams(vmem_limit_bytes=VMEM_LIMIT),
    )(x, lat, ext, w["wuv_pad"], w["conv_w"], w["conv_b"], w["ln_g"], w["ln_b"], w["gattn"],
      w["gconv"], w["wout"])


def _head_layout(g):
    return jnp.concatenate([g[QK_NOPE:], g[:QK_NOPE], jnp.zeros((HEAD_LANES - QK_DIM,), F32)])[None]


def _prep_weights(norm_mix_g, w_in, q_a_norm_g, w_uq, kv_a_norm_g, w_ukv, q_norm_g, k_norm_g,
                  conv_dw_w, conv_dw_b, conv_ln_g, conv_ln_b, out_norm_attn_g, out_norm_conv_g,
                  w_out, norm_ffn_g, w_ffn_up, ffn_dw_w, ffn_dw_b, w_ffn_down):
    half = QK_ROPE // 2
    o = 2 * D_CONV + Q_LORA + KV_LORA
    kp1, kp2 = w_in[:, o:o + half], w_in[:, o + half:o + QK_ROPE]
    win = jnp.concatenate([w_in, -kp2, kp1, jnp.zeros((D_MODEL, LANES - 2 * QK_ROPE), F32)], axis=1)

    uq = w_uq.reshape(Q_LORA, N_HEADS, QK_DIM)
    x1, x2 = uq[..., QK_NOPE:QK_NOPE + half], uq[..., QK_NOPE + half:]
    zq = jnp.zeros((Q_LORA, N_HEADS, HEAD_LANES - QK_DIM), F32)
    q_main = jnp.concatenate([x1, x2, uq[..., :QK_NOPE], zq], axis=-1)
    q_rot = jnp.concatenate([-x2, x1, jnp.zeros((Q_LORA, N_HEADS, HEAD_LANES - QK_ROPE), F32)], axis=-1)
    wuq = jnp.concatenate([q_main.reshape(Q_LORA, D_QK_PAD), q_rot.reshape(Q_LORA, D_QK_PAD)], axis=1)

    ukv = w_ukv.reshape(KV_LORA, N_HEADS, QK_NOPE + V_DIM)
    uk, uv = ukv[..., :QK_NOPE], ukv[..., QK_NOPE:]
    wuk = jnp.concatenate([jnp.zeros((KV_LORA, N_HEADS, QK_ROPE), F32), uk,
                           jnp.zeros((KV_LORA, N_HEADS, HEAD_LANES - QK_DIM), F32)],
                          axis=-1).reshape(KV_LORA, D_QK_PAD)
    wuvT = uv.reshape(KV_LORA, D_ATTN).T
    eye = jnp.eye(N_HEADS, dtype=F32)
    wuv_pad = jnp.einsum("chd,hg->hcgd", uv, eye).reshape(N_HEADS, KV_LORA, D_ATTN)
    a_nope = jnp.concatenate([uk.transpose(1, 2, 0).reshape(N_HEADS * QK_NOPE, KV_LORA),
                              jnp.zeros((N_HEADS * QK_NOPE, LANES), F32)], axis=1)
    a_rope = jnp.concatenate([jnp.zeros((QK_ROPE, KV_LORA), F32), jnp.eye(QK_ROPE, dtype=F32),
                              jnp.zeros((QK_ROPE, LANES - QK_ROPE), F32)], axis=1)
    aconst = jnp.concatenate([a_nope, a_rope], axis=0)

    interleave = _interleave
    scale = QK_DIM ** -0.5
    return dict(
        gmix=norm_mix_g[None], win=win.astype(BF16), gqa=q_a_norm_g[None], wuq=wuq.astype(BF16),
        gkva=kv_a_norm_g[None], wuk=wuk.astype(BF16), wuvT=wuvT.astype(BF16),
        gq=_head_layout(q_norm_g) * scale, gk=_head_layout(k_norm_g),
        wuv_pad=wuv_pad.astype(BF16), aconst=aconst.astype(BF16),
        conv_w=jnp.pad(conv_dw_w, ((0, CONV_HIST_PAD - CONV_WIDTH), (0, 0))), conv_b=conv_dw_b[None],
        ln_g=conv_ln_g[None], ln_b=conv_ln_b[None], gattn=out_norm_attn_g[None],
        gconv=out_norm_conv_g[None], wout=w_out.astype(BF16), gffn=norm_ffn_g[None],
        wup=interleave(w_ffn_up).astype(BF16),
        ffn_w=jnp.pad(interleave(ffn_dw_w), ((0, SUBLANES - FFN_CONV_WIDTH), (0, 0))),
        ffn_b=interleave(ffn_dw_b)[None], wdn=w_ffn_down.astype(BF16),
    )


def _interleave(a):
    nchunk = D_FF // FFN_CHUNK
    lead = a.shape[:-1]
    g = a[..., :D_FF].reshape(*lead, nchunk, FFN_CHUNK)
    u = a[..., D_FF:].reshape(*lead, nchunk, FFN_CHUNK)
    return jnp.stack([g, u], axis=-2).reshape(*lead, 2 * D_FF)


def _deinterleave(a):
    nchunk = D_FF // FFN_CHUNK
    lead = a.shape[:-1]
    a = a.reshape(*lead, nchunk, 2, FFN_CHUNK)
    return jnp.concatenate([a[..., 0, :].reshape(*lead, D_FF), a[..., 1, :].reshape(*lead, D_FF)], axis=-1)


def _rope_lanes(pos):
    inv_freq = ROPE_THETA ** (-jnp.arange(0, QK_ROPE, 2, dtype=F32) / QK_ROPE)
    ang = pos.astype(F32)[:, None] * inv_freq[None, :]
    pad = jnp.zeros((pos.shape[0], LANES - QK_ROPE), F32)
    cos = jnp.concatenate([jnp.cos(ang), jnp.cos(ang), pad], axis=1)
    sin = jnp.concatenate([jnp.sin(ang), jnp.sin(ang), pad], axis=1)
    return cos, sin


def _prompt_layer(x, pos, conv_hist, ffn_hist, k_prefix, vT_prefix, w, *, t, tq, n_valid=None):
    cos, sin = _rope_lanes(pos)
    glu, q, k, vT, ckv, kpe = _proj(x, cos, sin, w, t=t, tq=tq)
    ao = _attention(q, k, vT, k_prefix, vT_prefix, tq=tq)
    x1 = _convout(x, glu, conv_hist, ao, w, t=tq)
    y, ffn_state = _ffn(x1, ffn_hist, w, t=tq, ts=1, has_prev=True, n_valid=n_valid)
    return y, glu, k, vT, ckv, kpe, ffn_state


def kernel(x_prompt, x_sample, cache_ckv, cache_kpe, state_conv, state_ffn, page_table, meta_tokens,
           norm_mix_g, w_in, q_a_norm_g, w_uq, kv_a_norm_g, w_ukv, q_norm_g, k_norm_g, conv_dw_w,
           conv_dw_b, conv_ln_g, conv_ln_b, out_norm_attn_g, out_norm_conv_g, w_out, norm_ffn_g,
           w_ffn_up, ffn_dw_w, ffn_dw_b, w_ffn_down):
    assert norm_mix_g.shape[0] == 1, "single-layer step"
    w = _prep_weights(norm_mix_g[0], w_in[0], q_a_norm_g[0], w_uq[0], kv_a_norm_g[0], w_ukv[0],
                      q_norm_g[0], k_norm_g[0], conv_dw_w[0], conv_dw_b[0], conv_ln_g[0],
                      conv_ln_b[0], out_norm_attn_g[0], out_norm_conv_g[0], w_out[0], norm_ffn_g[0],
                      w_ffn_up[0], ffn_dw_w[0], ffn_dw_b[0], w_ffn_down[0])
    bp, seq, _ = x_prompt.shape
    tq = 256
    tp = 512
    assert seq % tp == 0

    xm = jnp.pad(meta_tokens.astype(F32), ((0, tq - N_META), (0, 0)))[None]
    zc = jnp.zeros((1, CONV_HIST_PAD, D_CONV), F32)
    zf = jnp.zeros((1, FFN_HIST_PAD, 2 * D_FF), F32)
    _, glu_m, k_m, vT_m, ckv_m, kpe_m, st_m = _prompt_layer(
        xm, jnp.arange(tq), zc, zf, None, None, w, t=tq, tq=tq, n_valid=N_META)

    conv_hist = jnp.pad(glu_m[:, :N_META], ((0, 0), (CONV_HIST_PAD - N_META, 0), (0, 0)))
    ffn_hist = jnp.pad(st_m, ((0, 0), (FFN_HIST_PAD - (FFN_CONV_WIDTH - 1), 0), (0, 0)))
    y_prompt, glu_p, _, _, ckv_p, kpe_p, st_p = _prompt_layer(
        x_prompt, N_META + jnp.arange(seq), conv_hist, ffn_hist,
        k_m[:, :N_META], vT_m[:, 0, :, :N_META], w, t=tp, tq=tq)
    ckv_prompt = jnp.concatenate(
        [jnp.broadcast_to(ckv_m[:, :N_META], (bp, N_META, KV_LORA)), ckv_p], axis=1)[None]
    kpe_prompt = jnp.concatenate(
        [jnp.broadcast_to(kpe_m[:, :N_META], (bp, N_META, QK_ROPE)), kpe_p], axis=1)[None]
    conv_prompt = glu_p[:, seq - (CONV_WIDTH - 1):][None]
    ffn_prompt = _deinterleave(st_p)[None]

    nb, ns, _ = x_sample.shape
    n = nb * ns
    n_hist = FFN_CONV_WIDTH - 1
    assert ns >= n_hist and ns <= SUBLANES
    page = cache_ckv.shape[2]
    past_len = page_table.shape[1] * page
    xs = x_sample.transpose(1, 0, 2).reshape(1, n, D_MODEL)
    cos_s, sin_s = _rope_lanes(past_len + jnp.arange(n) // nb)
    glu_s, q_s, _, _, ckv_s, kpe_s = _proj(xs, cos_s, sin_s, w, t=n, tq=n)

    def to_seq_major(a):
        return a.reshape(ns, nb, a.shape[-1]).transpose(1, 0, 2)

    qlat = _qlat(q_s[0], w)
    qlat = qlat.reshape(N_HEADS, ns, nb, 2 * LANES).transpose(2, 1, 0, 3).reshape(nb, ns * N_HEADS, 2 * LANES)
    ckv_sm, kpe_sm = to_seq_major(ckv_s[0]), to_seq_major(kpe_s[0])
    newc = jnp.pad(ckv_sm, ((0, 0), (0, LANES - ns), (0, 0)))
    newp = jnp.pad(kpe_sm, ((0, 0), (0, LANES - ns), (0, 0)))
    pages = 8 if page_table.shape[1] % 8 == 0 else 1
    lat = _sample_attention(page_table, cache_ckv, cache_kpe, newc, newp, w["aconst"], qlat, pages=pages)
    lat = lat.reshape(nb, ns, N_HEADS, KV_LORA).transpose(2, 1, 0, 3).reshape(N_HEADS, n, KV_LORA)
    ext = jnp.concatenate([state_conv[0].transpose(1, 0, 2), glu_s[0].reshape(ns, nb, D_CONV)], axis=0)
    x1_s = _sample_post(xs[0], lat, ext, w, nb=nb)
    hist_s = _interleave(state_ffn[0]).transpose(1, 0, 2).reshape(1, n_hist * nb, 2 * D_FF)
    y_s, st_s = _ffn(x1_s[None], hist_s, w, t=n, ts=nb, has_prev=False)

    y_sample = to_seq_major(y_s[0])
    conv_sample = jnp.concatenate([state_conv[0][:, ns:], to_seq_major(glu_s[0])], axis=1)[None]
    ffn_sample = _deinterleave(st_s[0]).reshape(n_hist, nb, 2 * D_FF).transpose(1, 0, 2)[None]
    return (y_prompt, y_sample, ckv_prompt, kpe_prompt, conv_prompt, ffn_prompt,
            ckv_sm[None], kpe_sm[None], conv_sample, ffn_sample)
```

```python
import functools

import jax
import jax.numpy as jnp
from jax import lax
from jax.experimental import pallas as pl
from jax.experimental.pallas import tpu as pltpu

F32 = jnp.float32
BF16 = jnp.bfloat16

D_MODEL = 1024
N_META = 16
D_CONV = 512
CONV_WIDTH = 31
N_HEADS = 8
QK_NOPE = 64
QK_ROPE = 32
QK_DIM = QK_NOPE + QK_ROPE
V_DIM = 64
D_ATTN = N_HEADS * V_DIM
Q_LORA = 256
KV_LORA = 128
ROPE_THETA = 10000.0
D_FF = 2816
FFN_CONV_WIDTH = 3
EPS = 1e-6

LANES = 128
SUBLANES = 8
HEAD_LANES = LANES
D_QK_PAD = N_HEADS * HEAD_LANES
D_IN_PAD = 2 * D_CONV + Q_LORA + KV_LORA + LANES
CONV_HIST_PAD = 32
FFN_HIST_PAD = 16
FFN_CHUNK = 256
V_ROWS = 80
LOG2E = 1.4426950408889634
ATTN_HEADS_PER_STEP = 4
SAMPLE_PAGES = 16
SAMPLE_GROUP = 4
VMEM_LIMIT = 56 * 1024 * 1024


def _cparams(sem):
    return pltpu.CompilerParams(dimension_semantics=sem, vmem_limit_bytes=VMEM_LIMIT)


def _rms_scale(x, n):
    return lax.rsqrt(jnp.sum(x * x, axis=-1, keepdims=True) * (1.0 / n) + EPS)


def _const_spec(shape):
    nd = len(shape)
    return pl.BlockSpec(shape, lambda *_: (0,) * nd)


def _proj_kernel(x_ref, cos_ref, sin_ref, gmix_ref, win_ref, gqa_ref, wuq_ref, gkva_ref, wuk_ref,
                 wuvT_ref, gq_ref, gk_ref, glu_ref, q_ref, k_ref, vT_ref, ckv_ref, kpe_ref, *, tq):
    x = x_ref[0]
    t = x.shape[0]
    h = (x * _rms_scale(x, D_MODEL) * gmix_ref[...]).astype(BF16)
    proj = jnp.dot(h, win_ref[...], preferred_element_type=F32)
    glu_ref[0] = proj[:, :D_CONV] * jax.nn.sigmoid(proj[:, D_CONV:2 * D_CONV])
    o = 2 * D_CONV
    cq = proj[:, o:o + Q_LORA]
    ckv_raw = proj[:, o + Q_LORA:o + Q_LORA + KV_LORA]
    kp = proj[:, o + Q_LORA + KV_LORA:]

    cqn = (cq * _rms_scale(cq, Q_LORA) * gqa_ref[...]).astype(BF16)
    qq = jnp.dot(cqn, wuq_ref[...], preferred_element_type=F32)
    ckv = ckv_raw * _rms_scale(ckv_raw, KV_LORA) * gkva_ref[...]
    ckv_ref[0] = ckv
    cos = cos_ref[...]
    sin = sin_ref[...]
    kpe = kp * cos + pltpu.roll(kp, LANES - QK_ROPE, 1) * sin
    kpe_ref[0] = kpe[:, :QK_ROPE]
    ckv_b = ckv.astype(BF16)
    kn = jnp.dot(ckv_b, wuk_ref[...], preferred_element_type=F32)
    vT = lax.dot_general(wuvT_ref[...], ckv_b, (((1,), (1,)), ((), ())),
                         preferred_element_type=F32).astype(BF16)
    pad_row = lax.broadcasted_iota(jnp.int32, (V_ROWS - V_DIM, tq), 0)
    ones_rows = jnp.where(pad_row == 0, 1.0, 0.0).astype(BF16)
    for j in range(t // tq):
        for hd in range(N_HEADS):
            vT_ref[0, j, hd * V_ROWS:hd * V_ROWS + V_DIM] = vT[hd * V_DIM:(hd + 1) * V_DIM,
                                                               j * tq:(j + 1) * tq]
            vT_ref[0, j, hd * V_ROWS + V_DIM:(hd + 1) * V_ROWS] = ones_rows

    lane = lax.broadcasted_iota(jnp.int32, (1, LANES), 1)
    nope = jnp.where((lane >= QK_ROPE) & (lane < QK_DIM), 1.0, 0.0).astype(F32)
    cosq = cos + nope
    gq = gq_ref[...]
    gk = gk_ref[...]
    for hd in range(N_HEADS):
        sl = slice(hd * HEAD_LANES, (hd + 1) * HEAD_LANES)
        qh = qq[:, sl] * cosq + qq[:, D_QK_PAD + hd * HEAD_LANES:D_QK_PAD + (hd + 1) * HEAD_LANES] * sin
        q_ref[0, :, sl] = (qh * _rms_scale(qh, QK_DIM) * gq).astype(BF16)
        kh = kn[:, sl] + kpe
        k_ref[0, :, sl] = (kh * _rms_scale(kh, QK_DIM) * gk).astype(BF16)


def _proj(x, cos, sin, w, *, t, tq):
    b, l, _ = x.shape
    kern = functools.partial(_proj_kernel, tq=tq)
    out_shape = (
        jax.ShapeDtypeStruct((b, l, D_CONV), F32),
        jax.ShapeDtypeStruct((b, l, D_QK_PAD), BF16),
        jax.ShapeDtypeStruct((b, l, D_QK_PAD), BF16),
        jax.ShapeDtypeStruct((b, l // tq, N_HEADS * V_ROWS, tq), BF16),
        jax.ShapeDtypeStruct((b, l, KV_LORA), F32),
        jax.ShapeDtypeStruct((b, l, QK_ROPE), F32),
    )
    row = lambda bi, i: (bi, i, 0)
    in_specs = [
        pl.BlockSpec((1, t, D_MODEL), row),
        pl.BlockSpec((t, LANES), lambda bi, i: (i, 0)),
        pl.BlockSpec((t, LANES), lambda bi, i: (i, 0)),
        _const_spec((1, D_MODEL)),
        _const_spec((D_MODEL, D_IN_PAD)),
        _const_spec((1, Q_LORA)),
        _const_spec((Q_LORA, 2 * D_QK_PAD)),
        _const_spec((1, KV_LORA)),
        _const_spec((KV_LORA, D_QK_PAD)),
        _const_spec((D_ATTN, KV_LORA)),
        _const_spec((1, LANES)),
        _const_spec((1, LANES)),
    ]
    out_specs = (
        pl.BlockSpec((1, t, D_CONV), row),
        pl.BlockSpec((1, t, D_QK_PAD), row),
        pl.BlockSpec((1, t, D_QK_PAD), row),
        pl.BlockSpec((1, t // tq, N_HEADS * V_ROWS, tq), lambda bi, i: (bi, i, 0, 0)),
        pl.BlockSpec((1, t, KV_LORA), row),
        pl.BlockSpec((1, t, QK_ROPE), row),
    )
    return pl.pallas_call(
        kern, name=f"proj_b{b}", grid=(b, l // t), in_specs=in_specs, out_specs=out_specs,
        out_shape=out_shape,
        compiler_params=_cparams(("parallel", "parallel")),
    )(x, cos, sin, w["gmix"], w["win"], w["gqa"], w["wuq"], w["gkva"], w["wuk"], w["wuvT"],
      w["gq"], w["gk"])


def _softmax_update(m, acc, s, vT):
    m_new = jnp.maximum(m, jnp.max(s, axis=0, keepdims=True))
    p = jnp.exp2(s - m_new).astype(BF16)
    acc = acc * jnp.exp2(m - m_new) + jnp.dot(vT, p, preferred_element_type=F32)
    return m_new, acc


def _attn_kernel(*refs, tq, nq, n_prefix, nh):
    if n_prefix:
        q_ref, k_ref, vT_ref, kp_ref, vTp_ref, o_ref, oT_ref = refs
    else:
        q_ref, k_ref, vT_ref, o_ref, oT_ref = refs
    tb = (((1,), (1,)), ((), ()))
    row_i = lax.broadcasted_iota(jnp.int32, (tq, tq), 0)
    col_i = lax.broadcasted_iota(jnp.int32, (tq, tq), 1)
    causal = row_i <= col_i
    heads = range(nh)
    lanes = [slice(h * HEAD_LANES, (h + 1) * HEAD_LANES) for h in heads]
    vrows = [slice(h * V_ROWS, (h + 1) * V_ROWS) for h in heads]

    def q_body(qi, carry):
        q0 = pl.multiple_of(qi * tq, tq)
        q_t = [q_ref[0, pl.ds(q0, tq), lanes[h]] for h in heads]
        state = []
        for h in heads:
            if n_prefix:
                s0 = lax.dot_general(kp_ref[0, :, lanes[h]], q_t[h], tb, preferred_element_type=F32)
                m = jnp.max(s0, axis=0, keepdims=True)
                acc = jnp.dot(vTp_ref[0, vrows[h], :], jnp.exp2(s0 - m).astype(BF16),
                              preferred_element_type=F32)
            else:
                m = jnp.full((1, tq), -jnp.inf, F32)
                acc = jnp.zeros((V_ROWS, tq), F32)
            state += [m, acc]

        def k_body(kj, c):
            k0 = pl.multiple_of(kj * tq, tq)
            out = []
            for h in heads:
                s = lax.dot_general(k_ref[0, pl.ds(k0, tq), lanes[h]], q_t[h], tb,
                                    preferred_element_type=F32)
                out += _softmax_update(c[2 * h], c[2 * h + 1], s, vT_ref[0, kj, vrows[h], :])
            return tuple(out)

        state = lax.fori_loop(0, qi, k_body, tuple(state))
        for h in heads:
            s = lax.dot_general(k_ref[0, pl.ds(q0, tq), lanes[h]], q_t[h], tb,
                                preferred_element_type=F32)
            s = jnp.where(causal, s, -jnp.inf)
            _, acc = _softmax_update(state[2 * h], state[2 * h + 1], s, vT_ref[0, qi, vrows[h], :])
            oT_ref[qi, h * V_DIM:(h + 1) * V_DIM, :] = acc[:V_DIM] / acc[V_DIM:V_DIM + 1]
        return carry

    lax.fori_loop(0, nq, q_body, 0)
    for qi in range(nq):
        o_ref[0, qi * tq:(qi + 1) * tq, :] = oT_ref[qi].T


def _attention(q, k, vT, k_prefix=None, vT_prefix=None, *, tq):
    b, l, _ = q.shape
    nq = l // tq
    n_prefix = 0 if k_prefix is None else k_prefix.shape[1]
    nh = ATTN_HEADS_PER_STEP
    kern = functools.partial(_attn_kernel, tq=tq, nq=nq, n_prefix=n_prefix, nh=nh)
    in_specs = [
        pl.BlockSpec((1, l, nh * HEAD_LANES), lambda bi, g: (bi, 0, g)),
        pl.BlockSpec((1, l, nh * HEAD_LANES), lambda bi, g: (bi, 0, g)),
        pl.BlockSpec((1, nq, nh * V_ROWS, tq), lambda bi, g: (bi, 0, g, 0)),
    ]
    args = [q, k, vT]
    if n_prefix:
        in_specs += [
            pl.BlockSpec((1, n_prefix, nh * HEAD_LANES), lambda bi, g: (0, 0, g)),
            pl.BlockSpec((1, nh * V_ROWS, n_prefix), lambda bi, g: (0, g, 0)),
        ]
        args += [k_prefix, vT_prefix]
    return pl.pallas_call(
        kern, name=f"attn_b{b}", grid=(b, N_HEADS // nh), in_specs=in_specs,
        out_specs=pl.BlockSpec((1, l, nh * V_DIM), lambda bi, g: (bi, 0, g)),
        out_shape=jax.ShapeDtypeStruct((b, l, D_ATTN), F32),
        scratch_shapes=[pltpu.VMEM((nq, nh * V_DIM, tq), F32)],
        compiler_params=_cparams(("parallel", "parallel")),
    )(*args)


def _conv_tail(y, lng, lnb, gconv):
    mu = jnp.mean(y, axis=-1, keepdims=True)
    yc = y - mu
    var = jnp.mean(yc * yc, axis=-1, keepdims=True)
    z = yc * lax.rsqrt(var + EPS) * lng + lnb
    z = z * jax.nn.sigmoid(z)
    return (z * _rms_scale(z, D_CONV) * gconv).astype(BF16)


def _convout_kernel(x_ref, glu_ref, prev_ref, hist_ref, ao_ref, cw_ref, cb_ref, lng_ref, lnb_ref,
                    gattn_ref, gconv_ref, wout_ref, o_ref, ext_ref, y_ref, *, rc):
    i = pl.program_id(1)
    t = glu_ref.shape[1]
    ext_ref[0:CONV_HIST_PAD] = jnp.where(i == 0, hist_ref[0], prev_ref[0])
    ext_ref[CONV_HIST_PAD:] = glu_ref[0]
    base = CONV_HIST_PAD - (CONV_WIDTH - 1)
    nt = rc // SUBLANES
    for cc in range(D_CONV // LANES):
        cs = slice(cc * LANES, (cc + 1) * LANES)
        accs = [jnp.zeros((nt, SUBLANES, LANES), F32) for _ in range(0, t, rc)]
        for res in range(SUBLANES):
            taps = [kk for kk in range(CONV_WIDTH) if (base + kk) % SUBLANES == res]
            shifted = ext_ref[res:base + taps[-1] + t, cs].reshape(-1, SUBLANES, LANES)
            for kk in taps:
                wk = cw_ref[kk, :, cs][None]
                off = (base + kk - res) // SUBLANES
                for a, r0 in enumerate(range(0, t, rc)):
                    lo = r0 // SUBLANES + off
                    accs[a] = accs[a] + wk * shifted[lo:lo + nt]
        for a, r0 in enumerate(range(0, t, rc)):
            y_ref[r0:r0 + rc, cs] = accs[a].reshape(rc, LANES)
    conv_n = _conv_tail(y_ref[...] + cb_ref[...], lng_ref[...], lnb_ref[...], gconv_ref[...])
    ao = ao_ref[0]
    attn_n = (ao * _rms_scale(ao, D_ATTN) * gattn_ref[...]).astype(BF16)
    upd = jnp.dot(attn_n, wout_ref[0:D_ATTN], preferred_element_type=F32)
    upd = upd + jnp.dot(conv_n, wout_ref[D_ATTN:], preferred_element_type=F32)
    o_ref[0] = x_ref[0] + upd


def _convout(x, glu, hist, ao, w, *, t):
    b, l, _ = x.shape
    per_batch_hist = hist.shape[0] > 1
    hb = t // CONV_HIST_PAD
    row = lambda bi, i: (bi, i, 0)
    in_specs = [
        pl.BlockSpec((1, t, D_MODEL), row),
        pl.BlockSpec((1, t, D_CONV), row),
        pl.BlockSpec((1, CONV_HIST_PAD, D_CONV), lambda bi, i: (bi, jnp.maximum(i * hb - 1, 0), 0)),
        pl.BlockSpec((1, CONV_HIST_PAD, D_CONV),
                     (lambda bi, i: (bi, 0, 0)) if per_batch_hist else (lambda bi, i: (0, 0, 0))),
        pl.BlockSpec((1, t, D_ATTN), row),
        _const_spec((CONV_WIDTH, SUBLANES, D_CONV)),
        _const_spec((1, D_CONV)),
        _const_spec((1, D_CONV)),
        _const_spec((1, D_CONV)),
        _const_spec((1, D_ATTN)),
        _const_spec((1, D_CONV)),
        _const_spec((D_MODEL, D_MODEL)),
    ]
    return pl.pallas_call(
        functools.partial(_convout_kernel, rc=min(t, 128)), name=f"convout_b{b}",
        grid=(b, l // t), in_specs=in_specs,
        out_specs=pl.BlockSpec((1, t, D_MODEL), row),
        out_shape=jax.ShapeDtypeStruct((b, l, D_MODEL), F32),
        scratch_shapes=[pltpu.VMEM((t + CONV_HIST_PAD, D_CONV), F32), pltpu.VMEM((t, D_CONV), F32)],
        compiler_params=_cparams(("parallel", "arbitrary")),
    )(x, glu, glu, hist, ao, w["conv_w8"], w["conv_b"], w["ln_g"], w["ln_b"], w["gattn"],
      w["gconv"], w["wout"])


def _ffn_kernel(*refs, ts, has_prev, st0):
    if has_prev:
        x_ref, prev_ref, hist_ref, g_ref, wup_ref, cw_ref, cb_ref, wdn_ref, o_ref, st_ref, act_ref = refs
    else:
        x_ref, hist_ref, g_ref, wup_ref, cw_ref, cb_ref, wdn_ref, o_ref, st_ref, act_ref = refs
    i = pl.program_id(1)
    x = x_ref[0]
    t = x.shape[0]
    hr = hist_ref.shape[1]
    g = g_ref[...]
    if has_prev:
        xe = jnp.concatenate([prev_ref[0], x], axis=0)
    else:
        xe = x
    hx = (xe * _rms_scale(xe, D_MODEL) * g).astype(BF16)
    cw = cw_ref[...]
    cb = cb_ref[...]
    w2 = 2 * FFN_CHUNK
    for c in range(D_FF // FFN_CHUNK):
        cs = slice(c * w2, (c + 1) * w2)
        u = jnp.dot(hx, wup_ref[:, cs], preferred_element_type=F32)
        hist = hist_ref[0, :, cs]
        if has_prev:
            u_ext = jnp.concatenate([jnp.where(i == 0, hist, u[:hr]), u[hr:]], axis=0)
        else:
            u_ext = jnp.concatenate([hist, u], axis=0)
        st_ref[0, :, cs] = u_ext[st0:st0 + 2 * ts]
        uc = (cw[0:1, cs] * u_ext[hr - 2 * ts:hr - 2 * ts + t]
              + cw[1:2, cs] * u_ext[hr - ts:hr - ts + t]
              + cw[2:3, cs] * u_ext[hr:] + cb[:, cs])
        gate = uc[:, :FFN_CHUNK]
        act = gate * jax.nn.sigmoid(gate) * uc[:, FFN_CHUNK:]
        act_ref[:, c * FFN_CHUNK:(c + 1) * FFN_CHUNK] = act.astype(BF16)
    o_ref[0] = x + jnp.dot(act_ref[...], wdn_ref[...], preferred_element_type=F32)


def _ffn(x, hist, w, *, t, ts, has_prev, n_valid=None):
    b, l, _ = x.shape
    hr = hist.shape[1]
    st0 = hr + (t if n_valid is None else n_valid) - 2 * ts
    per_batch_hist = hist.shape[0] > 1
    row = lambda bi, i: (bi, i, 0)
    in_specs = [pl.BlockSpec((1, t, D_MODEL), row)]
    args = [x]
    if has_prev:
        hb = t // hr
        in_specs.append(pl.BlockSpec((1, hr, D_MODEL), lambda bi, i: (bi, jnp.maximum(i * hb - 1, 0), 0)))
        args.append(x)
    in_specs += [
        pl.BlockSpec((1, hr, 2 * D_FF),
                     (lambda bi, i: (bi, 0, 0)) if per_batch_hist else (lambda bi, i: (0, 0, 0))),
        _const_spec((1, D_MODEL)),
        _const_spec((D_MODEL, 2 * D_FF)),
        _const_spec((SUBLANES, 2 * D_FF)),
        _const_spec((1, 2 * D_FF)),
        _const_spec((D_FF, D_MODEL)),
    ]
    args += [hist, w["gffn"], w["wup"], w["ffn_w"], w["ffn_b"], w["wdn"]]
    out, st = pl.pallas_call(
        functools.partial(_ffn_kernel, ts=ts, has_prev=has_prev, st0=st0), name=f"ffn_b{b}_ts{ts}",
        grid=(b, l // t), in_specs=in_specs,
        out_specs=(pl.BlockSpec((1, t, D_MODEL), row),
                   pl.BlockSpec((1, 2 * ts, 2 * D_FF), lambda bi, i: (bi, 0, 0))),
        out_shape=(jax.ShapeDtypeStruct((b, l, D_MODEL), F32),
                   jax.ShapeDtypeStruct((b, 2 * ts, 2 * D_FF), F32)),
        scratch_shapes=[pltpu.VMEM((t, D_FF), BF16)],
        compiler_params=_cparams(("parallel", "arbitrary")),
    )(*args)
    return out, st


def _qlat_kernel(q_ref, gk_ref, wuk_ref, o_ref):
    gk = gk_ref[...]
    n = q_ref.shape[0]
    for hd in range(N_HEADS):
        sl = slice(hd * HEAD_LANES, (hd + 1) * HEAD_LANES)
        qg = q_ref[:, sl].astype(F32) * gk
        lat = lax.dot_general(qg.astype(BF16), wuk_ref[:, sl], (((1,), (1,)), ((), ())),
                              preferred_element_type=F32)
        lane = lax.broadcasted_iota(jnp.int32, (1, LANES), 1)
        rope = jnp.where(lane < QK_ROPE, qg, 0.0)
        o_ref[hd] = jnp.concatenate([lat, rope], axis=1).astype(BF16)


def _qlat(q, w):
    n = q.shape[0]
    return pl.pallas_call(
        _qlat_kernel, name="sample_qlat",
        out_shape=jax.ShapeDtypeStruct((N_HEADS, n, 2 * LANES), BF16),
    )(q, w["gk"], w["wuk"])


def _sattn_kernel(pt_ref, *refs, pages, group, page, n_new):
    del pt_ref
    ckv_refs = refs[:pages]
    kpe_refs = refs[pages:2 * pages]
    (newc_ref, newp_ref, aconst_ref, qlat_ref, o_ref, a_ref, m_ref, l_ref, acc_ref) = refs[2 * pages:]
    c = pl.program_id(1)
    nconst = aconst_ref.shape[0]
    tb = (((1,), (1,)), ((), ()))

    @pl.when(c == 0)
    def _():
        a_ref[0:nconst] = aconst_ref[...]
        a_ref[nconst:] = qlat_ref[0, :, 0:KV_LORA]
        m_ref[...] = jnp.full(m_ref.shape, -jnp.inf, F32)
        l_ref[...] = jnp.zeros(l_ref.shape, F32)
        acc_ref[...] = jnp.zeros(acc_ref.shape, F32)

    q_rope = qlat_ref[0, :, KV_LORA:KV_LORA + QK_ROPE]

    def chain(ckv_b, kpe_t, mask=None):
        tk = ckv_b.shape[0]
        r = lax.dot_general(a_ref[...], ckv_b, tb, preferred_element_type=F32)
        kn = r[0:nconst]
        ss = jnp.sum((kn * kn).reshape(N_HEADS, QK_NOPE, tk), axis=1)
        ss = ss + jnp.sum(kpe_t * kpe_t, axis=0, keepdims=True)
        rinv = lax.rsqrt(ss * (1.0 / QK_DIM) + EPS)
        sc = r[nconst:] + jnp.dot(q_rope, kpe_t.astype(BF16), preferred_element_type=F32)
        s = sc * jnp.concatenate([rinv] * n_new, axis=0)
        if mask is not None:
            s = jnp.where(mask, s, -jnp.inf)
        m = jnp.max(s, axis=1, keepdims=True)
        p = jnp.exp2(s - m)
        return m, jnp.sum(p, axis=1, keepdims=True), jnp.dot(p.astype(BF16), ckv_b,
                                                             preferred_element_type=F32)

    def merge(parts):
        m_old = m_ref[...]
        m_new = m_old
        for m, _, _ in parts:
            m_new = jnp.maximum(m_new, m)
        w_old = jnp.exp2(m_old - m_new)
        l = l_ref[...] * w_old
        acc = acc_ref[...] * w_old
        for m, lp, ap in parts:
            wp = jnp.exp2(m - m_new)
            l = l + lp * wp
            acc = acc + ap * wp
        m_ref[...] = m_new
        l_ref[...] = l
        acc_ref[...] = acc

    parts = []
    for g0 in range(0, pages, group):
        ckv_b = jnp.concatenate([ckv_refs[j][0, 0].astype(BF16) for j in range(g0, g0 + group)], axis=0)
        kpe_t = jnp.concatenate([kpe_refs[j][0, 0] for j in range(g0, g0 + group)], axis=1)
        parts.append(chain(ckv_b, kpe_t))
    merge(parts)

    @pl.when(c == pl.num_programs(1) - 1)
    def _():
        rows = a_ref.shape[0] - nconst
        tok = lax.broadcasted_iota(jnp.int32, (rows, page), 1)
        step = lax.broadcasted_iota(jnp.int32, (rows, page), 0) // N_HEADS
        merge([chain(newc_ref[0].astype(BF16), newp_ref[0], mask=tok <= step)])
        o_ref[0] = acc_ref[...] / l_ref[...]


def _sample_attention(page_table, cache_ckv, cache_kpe_t, newc, newp_t, aconst, qlat, *, pages, group):
    nb, n_pages = page_table.shape
    page = cache_ckv.shape[2]
    nrow = qlat.shape[1]
    n_new = nrow // N_HEADS
    nconst = aconst.shape[0]

    def page_map(j):
        return lambda b, c, pt: (0, pt[b, c * pages + j], 0, 0)

    in_specs = [pl.BlockSpec((1, 1, page, KV_LORA), page_map(j)) for j in range(pages)]
    in_specs += [pl.BlockSpec((1, 1, QK_ROPE, page), page_map(j)) for j in range(pages)]
    in_specs += [
        pl.BlockSpec((1, page, KV_LORA), lambda b, c, pt: (b, 0, 0)),
        pl.BlockSpec((1, QK_ROPE, page), lambda b, c, pt: (b, 0, 0)),
        pl.BlockSpec((nconst, KV_LORA), lambda b, c, pt: (0, 0)),
        pl.BlockSpec((1, nrow, 2 * LANES), lambda b, c, pt: (b, 0, 0)),
    ]
    grid_spec = pltpu.PrefetchScalarGridSpec(
        num_scalar_prefetch=1, grid=(nb, n_pages // pages), in_specs=in_specs,
        out_specs=pl.BlockSpec((1, nrow, KV_LORA), lambda b, c, pt: (b, 0, 0)),
        scratch_shapes=[
            pltpu.VMEM((nconst + nrow, KV_LORA), BF16),
            pltpu.VMEM((nrow, 1), F32),
            pltpu.VMEM((nrow, 1), F32),
            pltpu.VMEM((nrow, KV_LORA), F32),
        ])
    return pl.pallas_call(
        functools.partial(_sattn_kernel, pages=pages, group=group, page=page, n_new=n_new),
        name="sample_attn", grid_spec=grid_spec,
        out_shape=jax.ShapeDtypeStruct((nb, nrow, KV_LORA), F32),
        compiler_params=_cparams(("arbitrary", "arbitrary")),
    )(page_table, *([cache_ckv] * pages), *([cache_kpe_t] * pages), newc, newp_t, aconst, qlat)


def _spost_kernel(x_ref, lat_ref, ext_ref, wuvp_ref, cw_ref, cb_ref, lng_ref, lnb_ref, gattn_ref,
                  gconv_ref, wout_ref, o_ref, y_ref, *, nb):
    n = x_ref.shape[0]
    n_new = n // nb
    ao = jnp.zeros((n, D_ATTN), F32)
    for hd in range(N_HEADS):
        ao = ao + jnp.dot(lat_ref[hd].astype(BF16), wuvp_ref[hd], preferred_element_type=F32)
    attn_n = (ao * _rms_scale(ao, D_ATTN) * gattn_ref[...]).astype(BF16)
    cw = cw_ref[...]
    for s in range(n_new):
        acc = jnp.zeros((nb, D_CONV), F32)
        for kk in range(CONV_WIDTH):
            acc = acc + cw[kk:kk + 1] * ext_ref[s + kk]
        y_ref[s * nb:(s + 1) * nb] = acc
    conv_n = _conv_tail(y_ref[...] + cb_ref[...], lng_ref[...], lnb_ref[...], gconv_ref[...])
    upd = jnp.dot(attn_n, wout_ref[0:D_ATTN], preferred_element_type=F32)
    upd = upd + jnp.dot(conv_n, wout_ref[D_ATTN:], preferred_element_type=F32)
    o_ref[...] = x_ref[...] + upd


def _sample_post(x, lat, ext, w, *, nb):
    n = x.shape[0]
    return pl.pallas_call(
        functools.partial(_spost_kernel, nb=nb), name="sample_post",
        out_shape=jax.ShapeDtypeStruct((n, D_MODEL), F32),
        scratch_shapes=[pltpu.VMEM((n, D_CONV), F32)],
        compiler_params=pltpu.CompilerParams(vmem_limit_bytes=VMEM_LIMIT),
    )(x, lat, ext, w["wuv_pad"], w["conv_w"], w["conv_b"], w["ln_g"], w["ln_b"], w["gattn"],
      w["gconv"], w["wout"])


def _head_layout(g):
    return jnp.concatenate([g[QK_NOPE:], g[:QK_NOPE], jnp.zeros((HEAD_LANES - QK_DIM,), F32)])[None]


def _prep_weights(norm_mix_g, w_in, q_a_norm_g, w_uq, kv_a_norm_g, w_ukv, q_norm_g, k_norm_g,
                  conv_dw_w, conv_dw_b, conv_ln_g, conv_ln_b, out_norm_attn_g, out_norm_conv_g,
                  w_out, norm_ffn_g, w_ffn_up, ffn_dw_w, ffn_dw_b, w_ffn_down):
    half = QK_ROPE // 2
    o = 2 * D_CONV + Q_LORA + KV_LORA
    kp1, kp2 = w_in[:, o:o + half], w_in[:, o + half:o + QK_ROPE]
    win = jnp.concatenate([w_in, -kp2, kp1, jnp.zeros((D_MODEL, LANES - 2 * QK_ROPE), F32)], axis=1)

    uq = w_uq.reshape(Q_LORA, N_HEADS, QK_DIM)
    x1, x2 = uq[..., QK_NOPE:QK_NOPE + half], uq[..., QK_NOPE + half:]
    zq = jnp.zeros((Q_LORA, N_HEADS, HEAD_LANES - QK_DIM), F32)
    q_main = jnp.concatenate([x1, x2, uq[..., :QK_NOPE], zq], axis=-1)
    q_rot = jnp.concatenate([-x2, x1, jnp.zeros((Q_LORA, N_HEADS, HEAD_LANES - QK_ROPE), F32)], axis=-1)
    wuq = jnp.concatenate([q_main.reshape(Q_LORA, D_QK_PAD), q_rot.reshape(Q_LORA, D_QK_PAD)], axis=1)

    ukv = w_ukv.reshape(KV_LORA, N_HEADS, QK_NOPE + V_DIM)
    uk, uv = ukv[..., :QK_NOPE], ukv[..., QK_NOPE:]
    wuk = jnp.concatenate([jnp.zeros((KV_LORA, N_HEADS, QK_ROPE), F32), uk,
                           jnp.zeros((KV_LORA, N_HEADS, HEAD_LANES - QK_DIM), F32)],
                          axis=-1).reshape(KV_LORA, D_QK_PAD)
    wuvT = uv.reshape(KV_LORA, D_ATTN).T
    eye = jnp.eye(N_HEADS, dtype=F32)
    wuv_pad = jnp.einsum("chd,hg->hcgd", uv, eye).reshape(N_HEADS, KV_LORA, D_ATTN)
    aconst = uk.transpose(1, 2, 0).reshape(N_HEADS * QK_NOPE, KV_LORA)

    interleave = _interleave
    scale = QK_DIM ** -0.5 * LOG2E
    return dict(
        gmix=norm_mix_g[None], win=win.astype(BF16), gqa=q_a_norm_g[None], wuq=wuq.astype(BF16),
        gkva=kv_a_norm_g[None], wuk=wuk.astype(BF16), wuvT=wuvT.astype(BF16),
        gq=_head_layout(q_norm_g) * scale, gk=_head_layout(k_norm_g),
        wuv_pad=wuv_pad.astype(BF16), aconst=aconst.astype(BF16),
        conv_w=jnp.pad(conv_dw_w, ((0, CONV_HIST_PAD - CONV_WIDTH), (0, 0))), conv_b=conv_dw_b[None],
        conv_w8=jnp.broadcast_to(conv_dw_w[:, None, :], (CONV_WIDTH, SUBLANES, D_CONV)),
        ln_g=conv_ln_g[None], ln_b=conv_ln_b[None], gattn=out_norm_attn_g[None],
        gconv=out_norm_conv_g[None], wout=w_out.astype(BF16), gffn=norm_ffn_g[None],
        wup=interleave(w_ffn_up).astype(BF16),
        ffn_w=jnp.pad(interleave(ffn_dw_w), ((0, SUBLANES - FFN_CONV_WIDTH), (0, 0))),
        ffn_b=interleave(ffn_dw_b)[None], wdn=w_ffn_down.astype(BF16),
    )


def _interleave(a):
    nchunk = D_FF // FFN_CHUNK
    lead = a.shape[:-1]
    g = a[..., :D_FF].reshape(*lead, nchunk, FFN_CHUNK)
    u = a[..., D_FF:].reshape(*lead, nchunk, FFN_CHUNK)
    return jnp.stack([g, u], axis=-2).reshape(*lead, 2 * D_FF)


def _deinterleave(a):
    nchunk = D_FF // FFN_CHUNK
    lead = a.shape[:-1]
    a = a.reshape(*lead, nchunk, 2, FFN_CHUNK)
    return jnp.concatenate([a[..., 0, :].reshape(*lead, D_FF), a[..., 1, :].reshape(*lead, D_FF)], axis=-1)


def _rope_lanes(pos):
    inv_freq = ROPE_THETA ** (-jnp.arange(0, QK_ROPE, 2, dtype=F32) / QK_ROPE)
    ang = pos.astype(F32)[:, None] * inv_freq[None, :]
    pad = jnp.zeros((pos.shape[0], LANES - QK_ROPE), F32)
    cos = jnp.concatenate([jnp.cos(ang), jnp.cos(ang), pad], axis=1)
    sin = jnp.concatenate([jnp.sin(ang), jnp.sin(ang), pad], axis=1)
    return cos, sin


def _prompt_layer(x, pos, conv_hist, ffn_hist, k_prefix, vT_prefix, w, *, t, tq, tr, n_valid=None):
    cos, sin = _rope_lanes(pos)
    glu, q, k, vT, ckv, kpe = _proj(x, cos, sin, w, t=t, tq=tq)
    ao = _attention(q, k, vT, k_prefix, vT_prefix, tq=tq)
    x1 = _convout(x, glu, conv_hist, ao, w, t=tr)
    y, ffn_state = _ffn(x1, ffn_hist, w, t=tr, ts=1, has_prev=True, n_valid=n_valid)
    return y, glu, k, vT, ckv, kpe, ffn_state


def kernel(x_prompt, x_sample, cache_ckv, cache_kpe, state_conv, state_ffn, page_table, meta_tokens,
           norm_mix_g, w_in, q_a_norm_g, w_uq, kv_a_norm_g, w_ukv, q_norm_g, k_norm_g, conv_dw_w,
           conv_dw_b, conv_ln_g, conv_ln_b, out_norm_attn_g, out_norm_conv_g, w_out, norm_ffn_g,
           w_ffn_up, ffn_dw_w, ffn_dw_b, w_ffn_down):
    assert norm_mix_g.shape[0] == 1, "single-layer step"
    w = _prep_weights(norm_mix_g[0], w_in[0], q_a_norm_g[0], w_uq[0], kv_a_norm_g[0], w_ukv[0],
                      q_norm_g[0], k_norm_g[0], conv_dw_w[0], conv_dw_b[0], conv_ln_g[0],
                      conv_ln_b[0], out_norm_attn_g[0], out_norm_conv_g[0], w_out[0], norm_ffn_g[0],
                      w_ffn_up[0], ffn_dw_w[0], ffn_dw_b[0], w_ffn_down[0])
    bp, seq, _ = x_prompt.shape
    tm = 256
    tp = 512
    tr = 256
    assert seq % tp == 0

    xm = jnp.pad(meta_tokens.astype(F32), ((0, tm - N_META), (0, 0)))[None]
    zc = jnp.zeros((1, CONV_HIST_PAD, D_CONV), F32)
    zf = jnp.zeros((1, FFN_HIST_PAD, 2 * D_FF), F32)
    _, glu_m, k_m, vT_m, ckv_m, kpe_m, st_m = _prompt_layer(
        xm, jnp.arange(tm), zc, zf, None, None, w, t=tm, tq=tm, tr=tm, n_valid=N_META)

    conv_hist = jnp.pad(glu_m[:, :N_META], ((0, 0), (CONV_HIST_PAD - N_META, 0), (0, 0)))
    ffn_hist = jnp.pad(st_m, ((0, 0), (FFN_HIST_PAD - (FFN_CONV_WIDTH - 1), 0), (0, 0)))
    y_prompt, glu_p, _, _, ckv_p, kpe_p, st_p = _prompt_layer(
        x_prompt, N_META + jnp.arange(seq), conv_hist, ffn_hist,
        k_m[:, :N_META], vT_m[:, 0, :, :N_META], w, t=tp, tq=tp, tr=tr)
    ckv_prompt = jnp.concatenate(
        [jnp.broadcast_to(ckv_m[:, :N_META], (bp, N_META, KV_LORA)), ckv_p], axis=1)[None]
    kpe_prompt = jnp.concatenate(
        [jnp.broadcast_to(kpe_m[:, :N_META], (bp, N_META, QK_ROPE)), kpe_p], axis=1)[None]
    conv_prompt = glu_p[:, seq - (CONV_WIDTH - 1):][None]
    ffn_prompt = _deinterleave(st_p)[None]

    nb, ns, _ = x_sample.shape
    n = nb * ns
    n_hist = FFN_CONV_WIDTH - 1
    assert ns >= n_hist and ns <= SUBLANES
    page = cache_ckv.shape[2]
    past_len = page_table.shape[1] * page
    xs = x_sample.transpose(1, 0, 2).reshape(1, n, D_MODEL)
    cos_s, sin_s = _rope_lanes(past_len + jnp.arange(n) // nb)
    glu_s, q_s, _, _, ckv_s, kpe_s = _proj(xs, cos_s, sin_s, w, t=n, tq=n)

    def to_seq_major(a):
        return a.reshape(ns, nb, a.shape[-1]).transpose(1, 0, 2)

    qlat = _qlat(q_s[0], w)
    qlat = qlat.reshape(N_HEADS, ns, nb, 2 * LANES).transpose(2, 1, 0, 3).reshape(nb, ns * N_HEADS, 2 * LANES)
    ckv_sm, kpe_sm = to_seq_major(ckv_s[0]), to_seq_major(kpe_s[0])
    newc = jnp.pad(ckv_sm, ((0, 0), (0, page - ns), (0, 0)))
    newp_t = jnp.pad(kpe_sm.transpose(0, 2, 1), ((0, 0), (0, 0), (0, page - ns)))
    assert page_table.shape[1] % SAMPLE_PAGES == 0
    lat = _sample_attention(page_table, cache_ckv, jnp.swapaxes(cache_kpe, 2, 3), newc, newp_t,
                            w["aconst"], qlat, pages=SAMPLE_PAGES, group=SAMPLE_GROUP)
    lat = lat.reshape(nb, ns, N_HEADS, KV_LORA).transpose(2, 1, 0, 3).reshape(N_HEADS, n, KV_LORA)
    ext = jnp.concatenate([state_conv[0].transpose(1, 0, 2), glu_s[0].reshape(ns, nb, D_CONV)], axis=0)
    x1_s = _sample_post(xs[0], lat, ext, w, nb=nb)
    hist_s = _interleave(state_ffn[0]).transpose(1, 0, 2).reshape(1, n_hist * nb, 2 * D_FF)
    y_s, st_s = _ffn(x1_s[None], hist_s, w, t=n, ts=nb, has_prev=False)

    y_sample = to_seq_major(y_s[0])
    conv_sample = jnp.concatenate([state_conv[0][:, ns:], to_seq_major(glu_s[0])], axis=1)[None]
    ffn_sample = _deinterleave(st_s[0]).reshape(n_hist, nb, 2 * D_FF).transpose(1, 0, 2)[None]
    return (y_prompt, y_sample, ckv_prompt, kpe_prompt, conv_prompt, ffn_prompt,
            ckv_sm[None], kpe_sm[None], conv_sample, ffn_sample)
```

```python
import functools

import jax
import jax.numpy as jnp
from jax import lax
from jax.experimental import pallas as pl
from jax.experimental.pallas import tpu as pltpu

F32 = jnp.float32
BF16 = jnp.bfloat16

D_MODEL = 1024
N_META = 16
D_CONV = 512
CONV_WIDTH = 31
N_HEADS = 8
QK_NOPE = 64
QK_ROPE = 32
QK_DIM = QK_NOPE + QK_ROPE
V_DIM = 64
D_ATTN = N_HEADS * V_DIM
Q_LORA = 256
KV_LORA = 128
ROPE_THETA = 10000.0
D_FF = 2816
FFN_CONV_WIDTH = 3
EPS = 1e-6

LANES = 128
SUBLANES = 8
HEAD_LANES = LANES
D_QK_PAD = N_HEADS * HEAD_LANES
D_IN_PAD = 2 * D_CONV + Q_LORA + KV_LORA + LANES
CONV_HIST_PAD = 32
FFN_HIST_PAD = 16
FFN_CHUNK = 256
V_ROWS = 80
LOG2E = 1.4426950408889634
ATTN_HEADS_PER_STEP = 4
SAMPLE_GROUP = 4
SAMPLE_SLOTS = 8
VMEM_LIMIT = 56 * 1024 * 1024


def _cparams(sem):
    return pltpu.CompilerParams(dimension_semantics=sem, vmem_limit_bytes=VMEM_LIMIT)


def _rms_scale(x, n):
    return lax.rsqrt(jnp.sum(x * x, axis=-1, keepdims=True) * (1.0 / n) + EPS)


def _const_spec(shape):
    nd = len(shape)
    return pl.BlockSpec(shape, lambda *_: (0,) * nd)


def _proj_kernel(x_ref, cos_ref, sin_ref, gmix_ref, win_ref, gqa_ref, wuq_ref, gkva_ref, wuk_ref,
                 wuvT_ref, gq_ref, gk_ref, glu_ref, q_ref, k_ref, vT_ref, ckv_ref, kpe_ref, *, tq):
    x = x_ref[0]
    t = x.shape[0]
    h = (x * _rms_scale(x, D_MODEL) * gmix_ref[...]).astype(BF16)
    proj = jnp.dot(h, win_ref[...], preferred_element_type=F32)
    glu_ref[0] = proj[:, :D_CONV] * jax.nn.sigmoid(proj[:, D_CONV:2 * D_CONV])
    o = 2 * D_CONV
    cq = proj[:, o:o + Q_LORA]
    ckv_raw = proj[:, o + Q_LORA:o + Q_LORA + KV_LORA]
    kp = proj[:, o + Q_LORA + KV_LORA:]

    cqn = (cq * _rms_scale(cq, Q_LORA) * gqa_ref[...]).astype(BF16)
    qq = jnp.dot(cqn, wuq_ref[...], preferred_element_type=F32)
    ckv = ckv_raw * _rms_scale(ckv_raw, KV_LORA) * gkva_ref[...]
    ckv_ref[0] = ckv
    cos = cos_ref[...]
    sin = sin_ref[...]
    kpe = kp * cos + pltpu.roll(kp, LANES - QK_ROPE, 1) * sin
    kpe_ref[0] = kpe[:, :QK_ROPE]
    ckv_b = ckv.astype(BF16)
    kn = jnp.dot(ckv_b, wuk_ref[...], preferred_element_type=F32)
    vT = lax.dot_general(wuvT_ref[...], ckv_b, (((1,), (1,)), ((), ())),
                         preferred_element_type=F32).astype(BF16)
    pad_row = lax.broadcasted_iota(jnp.int32, (V_ROWS - V_DIM, tq), 0)
    ones_rows = jnp.where(pad_row == 0, 1.0, 0.0).astype(BF16)
    for j in range(t // tq):
        for hd in range(N_HEADS):
            vT_ref[0, j, hd * V_ROWS:hd * V_ROWS + V_DIM] = vT[hd * V_DIM:(hd + 1) * V_DIM,
                                                               j * tq:(j + 1) * tq]
            vT_ref[0, j, hd * V_ROWS + V_DIM:(hd + 1) * V_ROWS] = ones_rows

    lane = lax.broadcasted_iota(jnp.int32, (1, LANES), 1)
    nope = jnp.where((lane >= QK_ROPE) & (lane < QK_DIM), 1.0, 0.0).astype(F32)
    cosq = cos + nope
    gq = gq_ref[...]
    gk = gk_ref[...]
    for hd in range(N_HEADS):
        sl = slice(hd * HEAD_LANES, (hd + 1) * HEAD_LANES)
        qh = qq[:, sl] * cosq + qq[:, D_QK_PAD + hd * HEAD_LANES:D_QK_PAD + (hd + 1) * HEAD_LANES] * sin
        q_ref[0, :, sl] = (qh * _rms_scale(qh, QK_DIM) * gq).astype(BF16)
        kh = kn[:, sl] + kpe
        k_ref[0, :, sl] = (kh * _rms_scale(kh, QK_DIM) * gk).astype(BF16)


def _proj(x, cos, sin, w, *, t, tq):
    b, l, _ = x.shape
    kern = functools.partial(_proj_kernel, tq=tq)
    out_shape = (
        jax.ShapeDtypeStruct((b, l, D_CONV), F32),
        jax.ShapeDtypeStruct((b, l, D_QK_PAD), BF16),
        jax.ShapeDtypeStruct((b, l, D_QK_PAD), BF16),
        jax.ShapeDtypeStruct((b, l // tq, N_HEADS * V_ROWS, tq), BF16),
        jax.ShapeDtypeStruct((b, l, KV_LORA), F32),
        jax.ShapeDtypeStruct((b, l, QK_ROPE), F32),
    )
    row = lambda bi, i: (bi, i, 0)
    in_specs = [
        pl.BlockSpec((1, t, D_MODEL), row),
        pl.BlockSpec((t, LANES), lambda bi, i: (i, 0)),
        pl.BlockSpec((t, LANES), lambda bi, i: (i, 0)),
        _const_spec((1, D_MODEL)),
        _const_spec((D_MODEL, D_IN_PAD)),
        _const_spec((1, Q_LORA)),
        _const_spec((Q_LORA, 2 * D_QK_PAD)),
        _const_spec((1, KV_LORA)),
        _const_spec((KV_LORA, D_QK_PAD)),
        _const_spec((D_ATTN, KV_LORA)),
        _const_spec((1, LANES)),
        _const_spec((1, LANES)),
    ]
    out_specs = (
        pl.BlockSpec((1, t, D_CONV), row),
        pl.BlockSpec((1, t, D_QK_PAD), row),
        pl.BlockSpec((1, t, D_QK_PAD), row),
        pl.BlockSpec((1, t // tq, N_HEADS * V_ROWS, tq), lambda bi, i: (bi, i, 0, 0)),
        pl.BlockSpec((1, t, KV_LORA), row),
        pl.BlockSpec((1, t, QK_ROPE), row),
    )
    return pl.pallas_call(
        kern, name=f"proj_b{b}", grid=(b, l // t), in_specs=in_specs, out_specs=out_specs,
        out_shape=out_shape,
        compiler_params=_cparams(("parallel", "parallel")),
    )(x, cos, sin, w["gmix"], w["win"], w["gqa"], w["wuq"], w["gkva"], w["wuk"], w["wuvT"],
      w["gq"], w["gk"])


def _softmax_update(m, acc, s, vT):
    m_new = jnp.maximum(m, jnp.max(s, axis=0, keepdims=True))
    p = jnp.exp2(s - m_new).astype(BF16)
    acc = acc * jnp.exp2(m - m_new) + jnp.dot(vT, p, preferred_element_type=F32)
    return m_new, acc


def _attn_kernel(*refs, tq, nq, n_prefix, nh):
    if n_prefix:
        q_ref, k_ref, vT_ref, kp_ref, vTp_ref, o_ref, oT_ref = refs
    else:
        q_ref, k_ref, vT_ref, o_ref, oT_ref = refs
    tb = (((1,), (1,)), ((), ()))
    row_i = lax.broadcasted_iota(jnp.int32, (tq, tq), 0)
    col_i = lax.broadcasted_iota(jnp.int32, (tq, tq), 1)
    causal = row_i <= col_i
    heads = range(nh)
    lanes = [slice(h * HEAD_LANES, (h + 1) * HEAD_LANES) for h in heads]
    vrows = [slice(h * V_ROWS, (h + 1) * V_ROWS) for h in heads]

    def q_body(qi, carry):
        q0 = pl.multiple_of(qi * tq, tq)
        q_t = [q_ref[0, pl.ds(q0, tq), lanes[h]] for h in heads]
        state = []
        for h in heads:
            if n_prefix:
                s0 = lax.dot_general(kp_ref[0, :, lanes[h]], q_t[h], tb, preferred_element_type=F32)
                m = jnp.max(s0, axis=0, keepdims=True)
                acc = jnp.dot(vTp_ref[0, vrows[h], :], jnp.exp2(s0 - m).astype(BF16),
                              preferred_element_type=F32)
            else:
                m = jnp.full((1, tq), -jnp.inf, F32)
                acc = jnp.zeros((V_ROWS, tq), F32)
            state += [m, acc]

        def scores(k0, h):
            return lax.dot_general(k_ref[0, pl.ds(k0, tq), lanes[h]], q_t[h], tb,
                                   preferred_element_type=F32)

        def k_body(kj, c):
            k0 = pl.multiple_of(kj * tq, tq)
            out = []
            s_next = scores(k0, 0)
            for h in heads:
                s = s_next
                if h + 1 < nh:
                    s_next = scores(k0, h + 1)
                out += _softmax_update(c[2 * h], c[2 * h + 1], s, vT_ref[0, kj, vrows[h], :])
            return tuple(out)

        state = lax.fori_loop(0, qi, k_body, tuple(state))
        s_next = scores(q0, 0)
        for h in heads:
            s = jnp.where(causal, s_next, -jnp.inf)
            if h + 1 < nh:
                s_next = scores(q0, h + 1)
            _, acc = _softmax_update(state[2 * h], state[2 * h + 1], s, vT_ref[0, qi, vrows[h], :])
            oT_ref[qi, h * V_DIM:(h + 1) * V_DIM, :] = acc[:V_DIM] / acc[V_DIM:V_DIM + 1]
        return carry

    lax.fori_loop(0, nq, q_body, 0)
    for qi in range(nq):
        o_ref[0, qi * tq:(qi + 1) * tq, :] = oT_ref[qi].T


def _attention(q, k, vT, k_prefix=None, vT_prefix=None, *, tq):
    b, l, _ = q.shape
    nq = l // tq
    n_prefix = 0 if k_prefix is None else k_prefix.shape[1]
    nh = ATTN_HEADS_PER_STEP
    kern = functools.partial(_attn_kernel, tq=tq, nq=nq, n_prefix=n_prefix, nh=nh)
    in_specs = [
        pl.BlockSpec((1, l, nh * HEAD_LANES), lambda bi, g: (bi, 0, g)),
        pl.BlockSpec((1, l, nh * HEAD_LANES), lambda bi, g: (bi, 0, g)),
        pl.BlockSpec((1, nq, nh * V_ROWS, tq), lambda bi, g: (bi, 0, g, 0)),
    ]
    args = [q, k, vT]
    if n_prefix:
        in_specs += [
            pl.BlockSpec((1, n_prefix, nh * HEAD_LANES), lambda bi, g: (0, 0, g)),
            pl.BlockSpec((1, nh * V_ROWS, n_prefix), lambda bi, g: (0, g, 0)),
        ]
        args += [k_prefix, vT_prefix]
    return pl.pallas_call(
        kern, name=f"attn_b{b}", grid=(b, N_HEADS // nh), in_specs=in_specs,
        out_specs=pl.BlockSpec((1, l, nh * V_DIM), lambda bi, g: (bi, 0, g)),
        out_shape=jax.ShapeDtypeStruct((b, l, D_ATTN), F32),
        scratch_shapes=[pltpu.VMEM((nq, nh * V_DIM, tq), F32)],
        compiler_params=_cparams(("parallel", "parallel")),
    )(*args)


def _conv_tail(y, lng, lnb, gconv):
    mu = jnp.mean(y, axis=-1, keepdims=True)
    yc = y - mu
    var = jnp.mean(yc * yc, axis=-1, keepdims=True)
    z = yc * lax.rsqrt(var + EPS) * lng + lnb
    z = z * jax.nn.sigmoid(z)
    return (z * _rms_scale(z, D_CONV) * gconv).astype(BF16)


def _convout_kernel(x_ref, glu_ref, prev_ref, hist_ref, ao_ref, cw_ref, cb_ref, lng_ref, lnb_ref,
                    gattn_ref, gconv_ref, wout_ref, o_ref, ext_ref, y_ref, *, rc):
    i = pl.program_id(1)
    t = glu_ref.shape[1]
    ext_ref[0:CONV_HIST_PAD] = jnp.where(i == 0, hist_ref[0], prev_ref[0])
    ext_ref[CONV_HIST_PAD:] = glu_ref[0]
    base = CONV_HIST_PAD - (CONV_WIDTH - 1)
    nt = rc // SUBLANES
    for cc in range(D_CONV // LANES):
        cs = slice(cc * LANES, (cc + 1) * LANES)
        accs = [jnp.zeros((nt, SUBLANES, LANES), F32) for _ in range(0, t, rc)]
        for res in range(SUBLANES):
            taps = [kk for kk in range(CONV_WIDTH) if (base + kk) % SUBLANES == res]
            shifted = ext_ref[res:base + taps[-1] + t, cs].reshape(-1, SUBLANES, LANES)
            for kk in taps:
                wk = cw_ref[kk, :, cs][None]
                off = (base + kk - res) // SUBLANES
                for a, r0 in enumerate(range(0, t, rc)):
                    lo = r0 // SUBLANES + off
                    accs[a] = accs[a] + wk * shifted[lo:lo + nt]
        for a, r0 in enumerate(range(0, t, rc)):
            y_ref[r0:r0 + rc, cs] = accs[a].reshape(rc, LANES)
    conv_n = _conv_tail(y_ref[...] + cb_ref[...], lng_ref[...], lnb_ref[...], gconv_ref[...])
    ao = ao_ref[0]
    attn_n = (ao * _rms_scale(ao, D_ATTN) * gattn_ref[...]).astype(BF16)
    upd = jnp.dot(attn_n, wout_ref[0:D_ATTN], preferred_element_type=F32)
    upd = upd + jnp.dot(conv_n, wout_ref[D_ATTN:], preferred_element_type=F32)
    o_ref[0] = x_ref[0] + upd


def _convout(x, glu, hist, ao, w, *, t):
    b, l, _ = x.shape
    per_batch_hist = hist.shape[0] > 1
    hb = t // CONV_HIST_PAD
    row = lambda bi, i: (bi, i, 0)
    in_specs = [
        pl.BlockSpec((1, t, D_MODEL), row),
        pl.BlockSpec((1, t, D_CONV), row),
        pl.BlockSpec((1, CONV_HIST_PAD, D_CONV), lambda bi, i: (bi, jnp.maximum(i * hb - 1, 0), 0)),
        pl.BlockSpec((1, CONV_HIST_PAD, D_CONV),
                     (lambda bi, i: (bi, 0, 0)) if per_batch_hist else (lambda bi, i: (0, 0, 0))),
        pl.BlockSpec((1, t, D_ATTN), row),
        _const_spec((CONV_WIDTH, SUBLANES, D_CONV)),
        _const_spec((1, D_CONV)),
        _const_spec((1, D_CONV)),
        _const_spec((1, D_CONV)),
        _const_spec((1, D_ATTN)),
        _const_spec((1, D_CONV)),
        _const_spec((D_MODEL, D_MODEL)),
    ]
    return pl.pallas_call(
        functools.partial(_convout_kernel, rc=min(t, 128)), name=f"convout_b{b}",
        grid=(b, l // t), in_specs=in_specs,
        out_specs=pl.BlockSpec((1, t, D_MODEL), row),
        out_shape=jax.ShapeDtypeStruct((b, l, D_MODEL), F32),
        scratch_shapes=[pltpu.VMEM((t + CONV_HIST_PAD, D_CONV), F32), pltpu.VMEM((t, D_CONV), F32)],
        compiler_params=_cparams(("parallel", "arbitrary")),
    )(x, glu, glu, hist, ao, w["conv_w8"], w["conv_b"], w["ln_g"], w["ln_b"], w["gattn"],
      w["gconv"], w["wout"])


def _ffn_kernel(*refs, ts, has_prev, st0):
    if has_prev:
        x_ref, prev_ref, hist_ref, g_ref, wup_ref, cw_ref, cb_ref, wdn_ref, o_ref, st_ref, act_ref = refs
    else:
        x_ref, hist_ref, g_ref, wup_ref, cw_ref, cb_ref, wdn_ref, o_ref, st_ref, act_ref = refs
    i = pl.program_id(1)
    x = x_ref[0]
    t = x.shape[0]
    hr = hist_ref.shape[1]
    g = g_ref[...]
    if has_prev:
        xe = jnp.concatenate([prev_ref[0], x], axis=0)
    else:
        xe = x
    hx = (xe * _rms_scale(xe, D_MODEL) * g).astype(BF16)
    cw = cw_ref[...]
    cb = cb_ref[...]
    w2 = 2 * FFN_CHUNK
    for c in range(D_FF // FFN_CHUNK):
        cs = slice(c * w2, (c + 1) * w2)
        u = jnp.dot(hx, wup_ref[:, cs], preferred_element_type=F32)
        hist = hist_ref[0, :, cs]
        if has_prev:
            u_ext = jnp.concatenate([jnp.where(i == 0, hist, u[:hr]), u[hr:]], axis=0)
        else:
            u_ext = jnp.concatenate([hist, u], axis=0)
        st_ref[0, :, cs] = u_ext[st0:st0 + 2 * ts]
        uc = (cw[0:1, cs] * u_ext[hr - 2 * ts:hr - 2 * ts + t]
              + cw[1:2, cs] * u_ext[hr - ts:hr - ts + t]
              + cw[2:3, cs] * u_ext[hr:] + cb[:, cs])
        gate = uc[:, :FFN_CHUNK]
        act = gate * jax.nn.sigmoid(gate) * uc[:, FFN_CHUNK:]
        act_ref[:, c * FFN_CHUNK:(c + 1) * FFN_CHUNK] = act.astype(BF16)
    o_ref[0] = x + jnp.dot(act_ref[...], wdn_ref[...], preferred_element_type=F32)


def _ffn(x, hist, w, *, t, ts, has_prev, n_valid=None):
    b, l, _ = x.shape
    hr = hist.shape[1]
    st0 = hr + (t if n_valid is None else n_valid) - 2 * ts
    per_batch_hist = hist.shape[0] > 1
    row = lambda bi, i: (bi, i, 0)
    in_specs = [pl.BlockSpec((1, t, D_MODEL), row)]
    args = [x]
    if has_prev:
        hb = t // hr
        in_specs.append(pl.BlockSpec((1, hr, D_MODEL), lambda bi, i: (bi, jnp.maximum(i * hb - 1, 0), 0)))
        args.append(x)
    in_specs += [
        pl.BlockSpec((1, hr, 2 * D_FF),
                     (lambda bi, i: (bi, 0, 0)) if per_batch_hist else (lambda bi, i: (0, 0, 0))),
        _const_spec((1, D_MODEL)),
        _const_spec((D_MODEL, 2 * D_FF)),
        _const_spec((SUBLANES, 2 * D_FF)),
        _const_spec((1, 2 * D_FF)),
        _const_spec((D_FF, D_MODEL)),
    ]
    args += [hist, w["gffn"], w["wup"], w["ffn_w"], w["ffn_b"], w["wdn"]]
    out, st = pl.pallas_call(
        functools.partial(_ffn_kernel, ts=ts, has_prev=has_prev, st0=st0), name=f"ffn_b{b}_ts{ts}",
        grid=(b, l // t), in_specs=in_specs,
        out_specs=(pl.BlockSpec((1, t, D_MODEL), row),
                   pl.BlockSpec((1, 2 * ts, 2 * D_FF), lambda bi, i: (bi, 0, 0))),
        out_shape=(jax.ShapeDtypeStruct((b, l, D_MODEL), F32),
                   jax.ShapeDtypeStruct((b, 2 * ts, 2 * D_FF), F32)),
        scratch_shapes=[pltpu.VMEM((t, D_FF), BF16)],
        compiler_params=_cparams(("parallel", "arbitrary")),
    )(*args)
    return out, st


def _qlat_kernel(q_ref, gk_ref, wuk_ref, o_ref):
    gk = gk_ref[...]
    n = q_ref.shape[0]
    for hd in range(N_HEADS):
        sl = slice(hd * HEAD_LANES, (hd + 1) * HEAD_LANES)
        qg = q_ref[:, sl].astype(F32) * gk
        lat = lax.dot_general(qg.astype(BF16), wuk_ref[:, sl], (((1,), (1,)), ((), ())),
                              preferred_element_type=F32)
        lane = lax.broadcasted_iota(jnp.int32, (1, LANES), 1)
        rope = jnp.where(lane < QK_ROPE, qg, 0.0)
        o_ref[hd] = jnp.concatenate([lat, rope], axis=1).astype(BF16)


def _qlat(q, w):
    n = q.shape[0]
    return pl.pallas_call(
        _qlat_kernel, name="sample_qlat",
        out_shape=jax.ShapeDtypeStruct((N_HEADS, n, 2 * LANES), BF16),
    )(q, w["gk"], w["wuk"])


def _sattn_kernel(pt_ref, ckv_hbm, kpe_hbm, newc_ref, newp_ref, aconst_ref, qlat_ref, o_ref,
                  cbuf, pbuf, sem, a_ref, *, group, page, n_new, n_groups, nslot):
    b = pl.program_id(0)
    nb = pl.num_programs(0)
    per_round = nslot // 2
    n_rounds = n_groups // per_round
    nconst = aconst_ref.shape[0]
    rows = a_ref.shape[0] - nconst
    tb = (((1,), (1,)), ((), ()))

    def group_copies(bb, g, slot):
        cps = []
        for j in range(group):
            pg = pt_ref[bb, g * group + j]
            cps.append(pltpu.make_async_copy(
                ckv_hbm.at[0, pg], cbuf.at[slot, pl.ds(j * page, page), :], sem.at[0, slot]))
            cps.append(pltpu.make_async_copy(
                kpe_hbm.at[0, pg], pbuf.at[slot, :, pl.ds(j * page, page)], sem.at[1, slot]))
        return cps

    def round_copies(bb, rd, half):
        return [cp for u in range(per_round)
                for cp in group_copies(bb, rd * per_round + u, half * per_round + u)]

    @pl.when(b == 0)
    def _():
        a_ref[0:nconst] = aconst_ref[...]
        for cp in round_copies(0, 0, 0):
            cp.start()

    a_ref[nconst:] = qlat_ref[0, :, 0:KV_LORA]
    q_rope = qlat_ref[0, :, KV_LORA:KV_LORA + QK_ROPE]

    def expand(ckv_f32):
        half = nconst // 2
        ckv_b = ckv_f32.astype(BF16)
        r0 = lax.dot_general(a_ref[0:half], ckv_b, tb, preferred_element_type=F32)
        r1 = lax.dot_general(a_ref[half:], ckv_b, tb, preferred_element_type=F32)
        return ckv_b, r0, r1

    def chain(expanded, kpe_t, mask=None):
        ckv_b, r0, r1 = expanded
        tk = ckv_b.shape[0]
        half = nconst // 2
        kn = jnp.concatenate([r0, r1[0:half]], axis=0)
        ss = jnp.sum((kn * kn).reshape(N_HEADS, QK_NOPE, tk), axis=1)
        ss = ss + jnp.sum(kpe_t * kpe_t, axis=0, keepdims=True)
        rinv = lax.rsqrt(ss * (1.0 / QK_DIM) + EPS)
        sc = r1[half:] + jnp.dot(q_rope, kpe_t.astype(BF16), preferred_element_type=F32)
        s = sc * jnp.concatenate([rinv] * n_new, axis=0)
        if mask is not None:
            s = jnp.where(mask, s, -jnp.inf)
        m = jnp.max(s, axis=1, keepdims=True)
        p = jnp.exp2(s - m)
        return m, jnp.sum(p, axis=1, keepdims=True), jnp.dot(p.astype(BF16), ckv_b,
                                                             preferred_element_type=F32)

    def merge(state, part):
        m_old, l_old, acc_old = state
        m, l, acc = part
        m_new = jnp.maximum(m_old, m)
        w_old = jnp.exp2(m_old - m_new)
        w = jnp.exp2(m - m_new)
        return m_new, l_old * w_old + l * w, acc_old * w_old + acc * w

    def two_rounds(it, state):
        for half in range(2):
            rd = it * 2 + half
            wrap = rd + 1 >= n_rounds
            bb = jnp.minimum(jnp.where(wrap, b + 1, b), nb - 1)
            for cp in round_copies(bb, jnp.where(wrap, 0, rd + 1), 1 - half):
                cp.start()
            for cp in round_copies(b, rd, half):
                cp.wait()
            base = half * per_round
            nxt = expand(cbuf[base])
            for u in range(per_round):
                cur = nxt
                if u + 1 < per_round:
                    nxt = expand(cbuf[base + u + 1])
                state = merge(state, chain(cur, pbuf[base + u]))
        return state

    state = (jnp.full((rows, 1), -jnp.inf, F32), jnp.zeros((rows, 1), F32),
             jnp.zeros((rows, KV_LORA), F32))
    state = lax.fori_loop(0, n_rounds // 2, two_rounds, state)

    tok = lax.broadcasted_iota(jnp.int32, (rows, page), 1)
    step = lax.broadcasted_iota(jnp.int32, (rows, page), 0) // N_HEADS
    _, l, acc = merge(state, chain(expand(newc_ref[0]), newp_ref[0], mask=tok <= step))
    o_ref[0] = acc / l

    @pl.when(b == nb - 1)
    def _():
        for cp in round_copies(b, 0, 0):
            cp.wait()


def _sample_attention(page_table, cache_ckv, cache_kpe_t, newc, newp_t, aconst, qlat, *, group, nslot):
    nb, n_pages = page_table.shape
    page = cache_ckv.shape[2]
    nrow = qlat.shape[1]
    n_new = nrow // N_HEADS
    nconst = aconst.shape[0]
    n_groups = n_pages // group
    assert n_pages % group == 0 and n_groups % nslot == 0
    in_specs = [
        pl.BlockSpec(memory_space=pl.ANY),
        pl.BlockSpec(memory_space=pl.ANY),
        pl.BlockSpec((1, page, KV_LORA), lambda b, pt: (b, 0, 0)),
        pl.BlockSpec((1, QK_ROPE, page), lambda b, pt: (b, 0, 0)),
        pl.BlockSpec((nconst, KV_LORA), lambda b, pt: (0, 0)),
        pl.BlockSpec((1, nrow, 2 * LANES), lambda b, pt: (b, 0, 0)),
    ]
    grid_spec = pltpu.PrefetchScalarGridSpec(
        num_scalar_prefetch=1, grid=(nb,), in_specs=in_specs,
        out_specs=pl.BlockSpec((1, nrow, KV_LORA), lambda b, pt: (b, 0, 0)),
        scratch_shapes=[
            pltpu.VMEM((nslot, group * page, KV_LORA), F32),
            pltpu.VMEM((nslot, QK_ROPE, group * page), F32),
            pltpu.SemaphoreType.DMA((2, nslot)),
            pltpu.VMEM((nconst + nrow, KV_LORA), BF16),
        ])
    return pl.pallas_call(
        functools.partial(_sattn_kernel, group=group, page=page, n_new=n_new, n_groups=n_groups,
                          nslot=nslot),
        name="sample_attn", grid_spec=grid_spec,
        out_shape=jax.ShapeDtypeStruct((nb, nrow, KV_LORA), F32),
        compiler_params=_cparams(("arbitrary",)),
    )(page_table, cache_ckv, cache_kpe_t, newc, newp_t, aconst, qlat)


def _spost_kernel(x_ref, lat_ref, ext_ref, wuvp_ref, cw_ref, cb_ref, lng_ref, lnb_ref, gattn_ref,
                  gconv_ref, wout_ref, o_ref, y_ref, *, nb):
    n = x_ref.shape[0]
    n_new = n // nb
    ao = jnp.zeros((n, D_ATTN), F32)
    for hd in range(N_HEADS):
        ao = ao + jnp.dot(lat_ref[hd].astype(BF16), wuvp_ref[hd], preferred_element_type=F32)
    attn_n = (ao * _rms_scale(ao, D_ATTN) * gattn_ref[...]).astype(BF16)
    cw = cw_ref[...]
    for s in range(n_new):
        acc = jnp.zeros((nb, D_CONV), F32)
        for kk in range(CONV_WIDTH):
            acc = acc + cw[kk:kk + 1] * ext_ref[s + kk]
        y_ref[s * nb:(s + 1) * nb] = acc
    conv_n = _conv_tail(y_ref[...] + cb_ref[...], lng_ref[...], lnb_ref[...], gconv_ref[...])
    upd = jnp.dot(attn_n, wout_ref[0:D_ATTN], preferred_element_type=F32)
    upd = upd + jnp.dot(conv_n, wout_ref[D_ATTN:], preferred_element_type=F32)
    o_ref[...] = x_ref[...] + upd


def _sample_post(x, lat, ext, w, *, nb):
    n = x.shape[0]
    return pl.pallas_call(
        functools.partial(_spost_kernel, nb=nb), name="sample_post",
        out_shape=jax.ShapeDtypeStruct((n, D_MODEL), F32),
        scratch_shapes=[pltpu.VMEM((n, D_CONV), F32)],
        compiler_params=pltpu.CompilerParams(vmem_limit_bytes=VMEM_LIMIT),
    )(x, lat, ext, w["wuv_pad"], w["conv_w"], w["conv_b"], w["ln_g"], w["ln_b"], w["gattn"],
      w["gconv"], w["wout"])


def _head_layout(g):
    return jnp.concatenate([g[QK_NOPE:], g[:QK_NOPE], jnp.zeros((HEAD_LANES - QK_DIM,), F32)])[None]


def _prep_weights(norm_mix_g, w_in, q_a_norm_g, w_uq, kv_a_norm_g, w_ukv, q_norm_g, k_norm_g,
                  conv_dw_w, conv_dw_b, conv_ln_g, conv_ln_b, out_norm_attn_g, out_norm_conv_g,
                  w_out, norm_ffn_g, w_ffn_up, ffn_dw_w, ffn_dw_b, w_ffn_down):
    half = QK_ROPE // 2
    o = 2 * D_CONV + Q_LORA + KV_LORA
    kp1, kp2 = w_in[:, o:o + half], w_in[:, o + half:o + QK_ROPE]
    win = jnp.concatenate([w_in, -kp2, kp1, jnp.zeros((D_MODEL, LANES - 2 * QK_ROPE), F32)], axis=1)

    uq = w_uq.reshape(Q_LORA, N_HEADS, QK_DIM)
    x1, x2 = uq[..., QK_NOPE:QK_NOPE + half], uq[..., QK_NOPE + half:]
    zq = jnp.zeros((Q_LORA, N_HEADS, HEAD_LANES - QK_DIM), F32)
    q_main = jnp.concatenate([x1, x2, uq[..., :QK_NOPE], zq], axis=-1)
    q_rot = jnp.concatenate([-x2, x1, jnp.zeros((Q_LORA, N_HEADS, HEAD_LANES - QK_ROPE), F32)], axis=-1)
    wuq = jnp.concatenate([q_main.reshape(Q_LORA, D_QK_PAD), q_rot.reshape(Q_LORA, D_QK_PAD)], axis=1)

    ukv = w_ukv.reshape(KV_LORA, N_HEADS, QK_NOPE + V_DIM)
    uk, uv = ukv[..., :QK_NOPE], ukv[..., QK_NOPE:]
    wuk = jnp.concatenate([jnp.zeros((KV_LORA, N_HEADS, QK_ROPE), F32), uk,
                           jnp.zeros((KV_LORA, N_HEADS, HEAD_LANES - QK_DIM), F32)],
                          axis=-1).reshape(KV_LORA, D_QK_PAD)
    wuvT = uv.reshape(KV_LORA, D_ATTN).T
    eye = jnp.eye(N_HEADS, dtype=F32)
    wuv_pad = jnp.einsum("chd,hg->hcgd", uv, eye).reshape(N_HEADS, KV_LORA, D_ATTN)
    aconst = uk.transpose(1, 2, 0).reshape(N_HEADS * QK_NOPE, KV_LORA)

    interleave = _interleave
    scale = QK_DIM ** -0.5 * LOG2E
    return dict(
        gmix=norm_mix_g[None], win=win.astype(BF16), gqa=q_a_norm_g[None], wuq=wuq.astype(BF16),
        gkva=kv_a_norm_g[None], wuk=wuk.astype(BF16), wuvT=wuvT.astype(BF16),
        gq=_head_layout(q_norm_g) * scale, gk=_head_layout(k_norm_g),
        wuv_pad=wuv_pad.astype(BF16), aconst=aconst.astype(BF16),
        conv_w=jnp.pad(conv_dw_w, ((0, CONV_HIST_PAD - CONV_WIDTH), (0, 0))), conv_b=conv_dw_b[None],
        conv_w8=jnp.broadcast_to(conv_dw_w[:, None, :], (CONV_WIDTH, SUBLANES, D_CONV)),
        ln_g=conv_ln_g[None], ln_b=conv_ln_b[None], gattn=out_norm_attn_g[None],
        gconv=out_norm_conv_g[None], wout=w_out.astype(BF16), gffn=norm_ffn_g[None],
        wup=interleave(w_ffn_up).astype(BF16),
        ffn_w=jnp.pad(interleave(ffn_dw_w), ((0, SUBLANES - FFN_CONV_WIDTH), (0, 0))),
        ffn_b=interleave(ffn_dw_b)[None], wdn=w_ffn_down.astype(BF16),
    )


def _interleave(a):
    nchunk = D_FF // FFN_CHUNK
    lead = a.shape[:-1]
    g = a[..., :D_FF].reshape(*lead, nchunk, FFN_CHUNK)
    u = a[..., D_FF:].reshape(*lead, nchunk, FFN_CHUNK)
    return jnp.stack([g, u], axis=-2).reshape(*lead, 2 * D_FF)


def _deinterleave(a):
    nchunk = D_FF // FFN_CHUNK
    lead = a.shape[:-1]
    a = a.reshape(*lead, nchunk, 2, FFN_CHUNK)
    return jnp.concatenate([a[..., 0, :].reshape(*lead, D_FF), a[..., 1, :].reshape(*lead, D_FF)], axis=-1)


def _rope_lanes(pos):
    inv_freq = ROPE_THETA ** (-jnp.arange(0, QK_ROPE, 2, dtype=F32) / QK_ROPE)
    ang = pos.astype(F32)[:, None] * inv_freq[None, :]
    pad = jnp.zeros((pos.shape[0], LANES - QK_ROPE), F32)
    cos = jnp.concatenate([jnp.cos(ang), jnp.cos(ang), pad], axis=1)
    sin = jnp.concatenate([jnp.sin(ang), jnp.sin(ang), pad], axis=1)
    return cos, sin


def _prompt_layer(x, pos, conv_hist, ffn_hist, k_prefix, vT_prefix, w, *, t, tq, tr, n_valid=None):
    cos, sin = _rope_lanes(pos)
    glu, q, k, vT, ckv, kpe = _proj(x, cos, sin, w, t=t, tq=tq)
    ao = _attention(q, k, vT, k_prefix, vT_prefix, tq=tq)
    x1 = _convout(x, glu, conv_hist, ao, w, t=tr)
    y, ffn_state = _ffn(x1, ffn_hist, w, t=tr, ts=1, has_prev=True, n_valid=n_valid)
    return y, glu, k, vT, ckv, kpe, ffn_state


def kernel(x_prompt, x_sample, cache_ckv, cache_kpe, state_conv, state_ffn, page_table, meta_tokens,
           norm_mix_g, w_in, q_a_norm_g, w_uq, kv_a_norm_g, w_ukv, q_norm_g, k_norm_g, conv_dw_w,
           conv_dw_b, conv_ln_g, conv_ln_b, out_norm_attn_g, out_norm_conv_g, w_out, norm_ffn_g,
           w_ffn_up, ffn_dw_w, ffn_dw_b, w_ffn_down):
    assert norm_mix_g.shape[0] == 1, "single-layer step"
    w = _prep_weights(norm_mix_g[0], w_in[0], q_a_norm_g[0], w_uq[0], kv_a_norm_g[0], w_ukv[0],
                      q_norm_g[0], k_norm_g[0], conv_dw_w[0], conv_dw_b[0], conv_ln_g[0],
                      conv_ln_b[0], out_norm_attn_g[0], out_norm_conv_g[0], w_out[0], norm_ffn_g[0],
                      w_ffn_up[0], ffn_dw_w[0], ffn_dw_b[0], w_ffn_down[0])
    bp, seq, _ = x_prompt.shape
    tm = 256
    tp = 512
    tr = 256
    assert seq % tp == 0

    xm = jnp.pad(meta_tokens.astype(F32), ((0, tm - N_META), (0, 0)))[None]
    zc = jnp.zeros((1, CONV_HIST_PAD, D_CONV), F32)
    zf = jnp.zeros((1, FFN_HIST_PAD, 2 * D_FF), F32)
    _, glu_m, k_m, vT_m, ckv_m, kpe_m, st_m = _prompt_layer(
        xm, jnp.arange(tm), zc, zf, None, None, w, t=tm, tq=tm, tr=tm, n_valid=N_META)

    conv_hist = jnp.pad(glu_m[:, :N_META], ((0, 0), (CONV_HIST_PAD - N_META, 0), (0, 0)))
    ffn_hist = jnp.pad(st_m, ((0, 0), (FFN_HIST_PAD - (FFN_CONV_WIDTH - 1), 0), (0, 0)))
    y_prompt, glu_p, _, _, ckv_p, kpe_p, st_p = _prompt_layer(
        x_prompt, N_META + jnp.arange(seq), conv_hist, ffn_hist,
        k_m[:, :N_META], vT_m[:, 0, :, :N_META], w, t=tp, tq=tp, tr=tr)
    ckv_prompt = jnp.concatenate(
        [jnp.broadcast_to(ckv_m[:, :N_META], (bp, N_META, KV_LORA)), ckv_p], axis=1)[None]
    kpe_prompt = jnp.concatenate(
        [jnp.broadcast_to(kpe_m[:, :N_META], (bp, N_META, QK_ROPE)), kpe_p], axis=1)[None]
    conv_prompt = glu_p[:, seq - (CONV_WIDTH - 1):][None]
    ffn_prompt = _deinterleave(st_p)[None]

    nb, ns, _ = x_sample.shape
    n = nb * ns
    n_hist = FFN_CONV_WIDTH - 1
    assert ns >= n_hist and ns <= SUBLANES
    page = cache_ckv.shape[2]
    past_len = page_table.shape[1] * page
    xs = x_sample.transpose(1, 0, 2).reshape(1, n, D_MODEL)
    cos_s, sin_s = _rope_lanes(past_len + jnp.arange(n) // nb)
    glu_s, q_s, _, _, ckv_s, kpe_s = _proj(xs, cos_s, sin_s, w, t=n, tq=n)

    def to_seq_major(a):
        return a.reshape(ns, nb, a.shape[-1]).transpose(1, 0, 2)

    qlat = _qlat(q_s[0], w)
    qlat = qlat.reshape(N_HEADS, ns, nb, 2 * LANES).transpose(2, 1, 0, 3).reshape(nb, ns * N_HEADS, 2 * LANES)
    ckv_sm, kpe_sm = to_seq_major(ckv_s[0]), to_seq_major(kpe_s[0])
    newc = jnp.pad(ckv_sm, ((0, 0), (0, page - ns), (0, 0)))
    newp_t = jnp.pad(kpe_sm.transpose(0, 2, 1), ((0, 0), (0, 0), (0, page - ns)))
    lat = _sample_attention(page_table, cache_ckv, jnp.swapaxes(cache_kpe, 2, 3), newc, newp_t,
                            w["aconst"], qlat, group=SAMPLE_GROUP, nslot=SAMPLE_SLOTS)
    lat = lat.reshape(nb, ns, N_HEADS, KV_LORA).transpose(2, 1, 0, 3).reshape(N_HEADS, n, KV_LORA)
    ext = jnp.concatenate([state_conv[0].transpose(1, 0, 2), glu_s[0].reshape(ns, nb, D_CONV)], axis=0)
    x1_s = _sample_post(xs[0], lat, ext, w, nb=nb)
    hist_s = _interleave(state_ffn[0]).transpose(1, 0, 2).reshape(1, n_hist * nb, 2 * D_FF)
    y_s, st_s = _ffn(x1_s[None], hist_s, w, t=n, ts=nb, has_prev=False)

    y_sample = to_seq_major(y_s[0])
    conv_sample = jnp.concatenate([state_conv[0][:, ns:], to_seq_major(glu_s[0])], axis=1)[None]
    ffn_sample = _deinterleave(st_s[0]).reshape(n_hist, nb, 2 * D_FF).transpose(1, 0, 2)[None]
    return (y_prompt, y_sample, ckv_prompt, kpe_prompt, conv_prompt, ffn_prompt,
            ckv_sm[None], kpe_sm[None], conv_sample, ffn_sample)
```

```python
import functools

import jax
import jax.numpy as jnp
from jax import lax
from jax.experimental import pallas as pl
from jax.experimental.pallas import tpu as pltpu

F32 = jnp.float32
BF16 = jnp.bfloat16

D_MODEL = 1024
N_META = 16
D_CONV = 512
CONV_WIDTH = 31
N_HEADS = 8
QK_NOPE = 64
QK_ROPE = 32
QK_DIM = QK_NOPE + QK_ROPE
V_DIM = 64
D_ATTN = N_HEADS * V_DIM
Q_LORA = 256
KV_LORA = 128
ROPE_THETA = 10000.0
D_FF = 2816
FFN_CONV_WIDTH = 3
EPS = 1e-6

LANES = 128
SUBLANES = 8
HEAD_LANES = LANES
D_QK_PAD = N_HEADS * HEAD_LANES
D_IN_PAD = 2 * D_CONV + Q_LORA + KV_LORA + LANES
CONV_HIST_PAD = 32
FFN_HIST_PAD = 16
FFN_CHUNK = 256
V_ROWS = 80
LOG2E = 1.4426950408889634
ATTN_HEADS_PER_STEP = 4
SAMPLE_GROUP = 4
SAMPLE_SLOTS = 16
VMEM_LIMIT = 56 * 1024 * 1024


def _cparams(sem):
    return pltpu.CompilerParams(dimension_semantics=sem, vmem_limit_bytes=VMEM_LIMIT)


def _rms_scale(x, n):
    return lax.rsqrt(jnp.sum(x * x, axis=-1, keepdims=True) * (1.0 / n) + EPS)


def _const_spec(shape):
    nd = len(shape)
    return pl.BlockSpec(shape, lambda *_: (0,) * nd)


def _proj_kernel(x_ref, cos_ref, sin_ref, gmix_ref, win_ref, gqa_ref, wuq_ref, gkva_ref, wuk_ref,
                 wuvT_ref, gq_ref, gk_ref, glu_ref, q_ref, k_ref, vT_ref, ckv_ref, kpe_ref, *, tq, th):
    t = x_ref.shape[1]
    lane = lax.broadcasted_iota(jnp.int32, (1, LANES), 1)
    nope = jnp.where((lane >= QK_ROPE) & (lane < QK_DIM), 1.0, 0.0).astype(F32)
    gq = gq_ref[...]
    gk = gk_ref[...]
    pad_row = lax.broadcasted_iota(jnp.int32, (V_ROWS - V_DIM, th), 0)
    ones_rows = jnp.where(pad_row == 0, 1.0, 0.0).astype(BF16)

    def in_proj(r0):
        x = x_ref[0, r0:r0 + th]
        h = (x * _rms_scale(x, D_MODEL) * gmix_ref[...]).astype(BF16)
        return jnp.dot(h, win_ref[...], preferred_element_type=F32)

    def low_rank(r0, proj):
        rows = slice(r0, r0 + th)
        glu_ref[0, rows] = proj[:, :D_CONV] * jax.nn.sigmoid(proj[:, D_CONV:2 * D_CONV])
        o = 2 * D_CONV
        cq = proj[:, o:o + Q_LORA]
        ckv_raw = proj[:, o + Q_LORA:o + Q_LORA + KV_LORA]
        kp = proj[:, o + Q_LORA + KV_LORA:]
        cqn = (cq * _rms_scale(cq, Q_LORA) * gqa_ref[...]).astype(BF16)
        qq = jnp.dot(cqn, wuq_ref[...], preferred_element_type=F32)
        ckv = ckv_raw * _rms_scale(ckv_raw, KV_LORA) * gkva_ref[...]
        ckv_ref[0, rows] = ckv
        cos = cos_ref[rows]
        sin = sin_ref[rows]
        kpe = kp * cos + pltpu.roll(kp, LANES - QK_ROPE, 1) * sin
        kpe_ref[0, rows] = kpe[:, :QK_ROPE]
        ckv_b = ckv.astype(BF16)
        kn = jnp.dot(ckv_b, wuk_ref[...], preferred_element_type=F32)
        vT = lax.dot_general(wuvT_ref[...], ckv_b, (((1,), (1,)), ((), ())),
                             preferred_element_type=F32).astype(BF16)
        j, c0 = divmod(r0, tq)
        for hd in range(N_HEADS):
            vT_ref[0, j, hd * V_ROWS:hd * V_ROWS + V_DIM, c0:c0 + th] = vT[hd * V_DIM:(hd + 1) * V_DIM]
            vT_ref[0, j, hd * V_ROWS + V_DIM:(hd + 1) * V_ROWS, c0:c0 + th] = ones_rows
        return qq, kn, kpe, cos, sin

    def heads_out(r0, qq, kn, kpe, cos, sin):
        rows = slice(r0, r0 + th)
        cosq = cos + nope
        for hd in range(N_HEADS):
            sl = slice(hd * HEAD_LANES, (hd + 1) * HEAD_LANES)
            qh = qq[:, sl] * cosq + qq[:, D_QK_PAD + hd * HEAD_LANES:D_QK_PAD + (hd + 1) * HEAD_LANES] * sin
            q_ref[0, rows, sl] = (qh * _rms_scale(qh, QK_DIM) * gq).astype(BF16)
            kh = kn[:, sl] + kpe
            k_ref[0, rows, sl] = (kh * _rms_scale(kh, QK_DIM) * gk).astype(BF16)

    starts = range(0, t, th)
    projs = [in_proj(r0) for r0 in starts]
    mids = [low_rank(r0, p) for r0, p in zip(starts, projs)]
    for r0, mid in zip(starts, mids):
        heads_out(r0, *mid)


def _proj(x, cos, sin, w, *, t, tq):
    b, l, _ = x.shape
    th = t // 2 if (t // 2) % LANES == 0 and tq % (t // 2) == 0 else t
    kern = functools.partial(_proj_kernel, tq=tq, th=th)
    out_shape = (
        jax.ShapeDtypeStruct((b, l, D_CONV), F32),
        jax.ShapeDtypeStruct((b, l, D_QK_PAD), BF16),
        jax.ShapeDtypeStruct((b, l, D_QK_PAD), BF16),
        jax.ShapeDtypeStruct((b, l // tq, N_HEADS * V_ROWS, tq), BF16),
        jax.ShapeDtypeStruct((b, l, KV_LORA), F32),
        jax.ShapeDtypeStruct((b, l, QK_ROPE), F32),
    )
    row = lambda bi, i: (bi, i, 0)
    in_specs = [
        pl.BlockSpec((1, t, D_MODEL), row),
        pl.BlockSpec((t, LANES), lambda bi, i: (i, 0)),
        pl.BlockSpec((t, LANES), lambda bi, i: (i, 0)),
        _const_spec((1, D_MODEL)),
        _const_spec((D_MODEL, D_IN_PAD)),
        _const_spec((1, Q_LORA)),
        _const_spec((Q_LORA, 2 * D_QK_PAD)),
        _const_spec((1, KV_LORA)),
        _const_spec((KV_LORA, D_QK_PAD)),
        _const_spec((D_ATTN, KV_LORA)),
        _const_spec((1, LANES)),
        _const_spec((1, LANES)),
    ]
    out_specs = (
        pl.BlockSpec((1, t, D_CONV), row),
        pl.BlockSpec((1, t, D_QK_PAD), row),
        pl.BlockSpec((1, t, D_QK_PAD), row),
        pl.BlockSpec((1, t // tq, N_HEADS * V_ROWS, tq), lambda bi, i: (bi, i, 0, 0)),
        pl.BlockSpec((1, t, KV_LORA), row),
        pl.BlockSpec((1, t, QK_ROPE), row),
    )
    return pl.pallas_call(
        kern, name=f"proj_b{b}", grid=(b, l // t), in_specs=in_specs, out_specs=out_specs,
        out_shape=out_shape,
        compiler_params=_cparams(("parallel", "parallel")),
    )(x, cos, sin, w["gmix"], w["win"], w["gqa"], w["wuq"], w["gkva"], w["wuk"], w["wuvT"],
      w["gq"], w["gk"])


def _softmax_update(m, acc, s, vT):
    m_new = jnp.maximum(m, jnp.max(s, axis=0, keepdims=True))
    p = jnp.exp2(s - m_new).astype(BF16)
    acc = acc * jnp.exp2(m - m_new) + jnp.dot(vT, p, preferred_element_type=F32)
    return m_new, acc


def _attn_kernel(*refs, tq, nq, n_prefix, nh):
    if n_prefix:
        q_ref, k_ref, vT_ref, kp_ref, vTp_ref, o_ref, oT_ref, s0_ref = refs
    else:
        q_ref, k_ref, vT_ref, o_ref, oT_ref, s0_ref = refs
    tb = (((1,), (1,)), ((), ()))
    row_i = lax.broadcasted_iota(jnp.int32, (tq, tq), 0)
    col_i = lax.broadcasted_iota(jnp.int32, (tq, tq), 1)
    causal = row_i <= col_i
    heads = range(nh)
    lanes = [slice(h * HEAD_LANES, (h + 1) * HEAD_LANES) for h in heads]
    vrows = [slice(h * V_ROWS, (h + 1) * V_ROWS) for h in heads]

    def q_body(qi, carry):
        q0 = pl.multiple_of(qi * tq, tq)
        q_t = [q_ref[0, pl.ds(q0, tq), lanes[h]] for h in heads]
        state = []
        for h in heads:
            if n_prefix:
                s0 = lax.dot_general(kp_ref[0, :, lanes[h]], q_t[h], tb, preferred_element_type=F32)
                m = jnp.max(s0, axis=0, keepdims=True)
                acc = jnp.dot(vTp_ref[0, vrows[h], :], jnp.exp2(s0 - m).astype(BF16),
                              preferred_element_type=F32)
            else:
                m = jnp.full((1, tq), -jnp.inf, F32)
                acc = jnp.zeros((V_ROWS, tq), F32)
            state += [m, acc]

        def scores(k0, h):
            return lax.dot_general(k_ref[0, pl.ds(k0, tq), lanes[h]], q_t[h], tb,
                                   preferred_element_type=F32)

        s0_ref[...] = scores(0, 0)

        def k_body(kj, c):
            k0 = pl.multiple_of(kj * tq, tq)
            out = []
            s_next = s0_ref[...]
            for h in heads:
                s = s_next
                if h + 1 < nh:
                    s_next = scores(k0, h + 1)
                else:
                    s0_ref[...] = scores(k0 + tq, 0)
                out += _softmax_update(c[2 * h], c[2 * h + 1], s, vT_ref[0, kj, vrows[h], :])
            return tuple(out)

        state = lax.fori_loop(0, qi, k_body, tuple(state))
        s_next = s0_ref[...]
        for h in heads:
            s = jnp.where(causal, s_next, -jnp.inf)
            if h + 1 < nh:
                s_next = scores(q0, h + 1)
            _, acc = _softmax_update(state[2 * h], state[2 * h + 1], s, vT_ref[0, qi, vrows[h], :])
            oT_ref[qi, h * V_DIM:(h + 1) * V_DIM, :] = acc[:V_DIM] / acc[V_DIM:V_DIM + 1]
        return carry

    lax.fori_loop(0, nq, q_body, 0)
    for qi in range(nq):
        o_ref[0, qi * tq:(qi + 1) * tq, :] = oT_ref[qi].T


def _attention(q, k, vT, k_prefix=None, vT_prefix=None, *, tq):
    b, l, _ = q.shape
    nq = l // tq
    n_prefix = 0 if k_prefix is None else k_prefix.shape[1]
    nh = ATTN_HEADS_PER_STEP
    kern = functools.partial(_attn_kernel, tq=tq, nq=nq, n_prefix=n_prefix, nh=nh)
    in_specs = [
        pl.BlockSpec((1, l, nh * HEAD_LANES), lambda bi, g: (bi, 0, g)),
        pl.BlockSpec((1, l, nh * HEAD_LANES), lambda bi, g: (bi, 0, g)),
        pl.BlockSpec((1, nq, nh * V_ROWS, tq), lambda bi, g: (bi, 0, g, 0)),
    ]
    args = [q, k, vT]
    if n_prefix:
        in_specs += [
            pl.BlockSpec((1, n_prefix, nh * HEAD_LANES), lambda bi, g: (0, 0, g)),
            pl.BlockSpec((1, nh * V_ROWS, n_prefix), lambda bi, g: (0, g, 0)),
        ]
        args += [k_prefix, vT_prefix]
    return pl.pallas_call(
        kern, name=f"attn_b{b}", grid=(b, N_HEADS // nh), in_specs=in_specs,
        out_specs=pl.BlockSpec((1, l, nh * V_DIM), lambda bi, g: (bi, 0, g)),
        out_shape=jax.ShapeDtypeStruct((b, l, D_ATTN), F32),
        scratch_shapes=[pltpu.VMEM((nq, nh * V_DIM, tq), F32), pltpu.VMEM((tq, tq), F32)],
        compiler_params=_cparams(("parallel", "parallel")),
    )(*args)


def _conv_tail(y, lng, lnb, gconv):
    mu = jnp.mean(y, axis=-1, keepdims=True)
    yc = y - mu
    var = jnp.mean(yc * yc, axis=-1, keepdims=True)
    z = yc * lax.rsqrt(var + EPS) * lng + lnb
    z = z * jax.nn.sigmoid(z)
    return (z * _rms_scale(z, D_CONV) * gconv).astype(BF16)


def _convout_kernel(x_ref, glu_ref, prev_ref, hist_ref, ao_ref, cw_ref, cb_ref, lng_ref, lnb_ref,
                    gattn_ref, gconv_ref, wout_ref, o_ref, ext_ref, y_ref, *, rc):
    i = pl.program_id(1)
    t = glu_ref.shape[1]
    ext_ref[0:CONV_HIST_PAD] = jnp.where(i == 0, hist_ref[0], prev_ref[0])
    ext_ref[CONV_HIST_PAD:] = glu_ref[0]
    base = CONV_HIST_PAD - (CONV_WIDTH - 1)
    nt = rc // SUBLANES
    for cc in range(D_CONV // LANES):
        cs = slice(cc * LANES, (cc + 1) * LANES)
        for r0 in range(0, t, rc):
            acc = jnp.zeros((nt, SUBLANES, LANES), F32)
            for res in range(SUBLANES):
                taps = [kk for kk in range(CONV_WIDTH) if (base + kk) % SUBLANES == res]
                shifted = ext_ref[r0 + res:r0 + base + taps[-1] + rc, cs].reshape(-1, SUBLANES, LANES)
                for kk in taps:
                    wk = cw_ref[kk, :, cs][None]
                    off = (base + kk - res) // SUBLANES
                    acc = acc + wk * shifted[off:off + nt]
            y_ref[r0:r0 + rc, cs] = acc.reshape(rc, LANES)
    conv_n = _conv_tail(y_ref[...] + cb_ref[...], lng_ref[...], lnb_ref[...], gconv_ref[...])
    ao = ao_ref[0]
    attn_n = (ao * _rms_scale(ao, D_ATTN) * gattn_ref[...]).astype(BF16)
    upd = jnp.dot(attn_n, wout_ref[0:D_ATTN], preferred_element_type=F32)
    upd = upd + jnp.dot(conv_n, wout_ref[D_ATTN:], preferred_element_type=F32)
    o_ref[0] = x_ref[0] + upd


def _convout(x, glu, hist, ao, w, *, t):
    b, l, _ = x.shape
    per_batch_hist = hist.shape[0] > 1
    hb = t // CONV_HIST_PAD
    row = lambda bi, i: (bi, i, 0)
    in_specs = [
        pl.BlockSpec((1, t, D_MODEL), row),
        pl.BlockSpec((1, t, D_CONV), row),
        pl.BlockSpec((1, CONV_HIST_PAD, D_CONV), lambda bi, i: (bi, jnp.maximum(i * hb - 1, 0), 0)),
        pl.BlockSpec((1, CONV_HIST_PAD, D_CONV),
                     (lambda bi, i: (bi, 0, 0)) if per_batch_hist else (lambda bi, i: (0, 0, 0))),
        pl.BlockSpec((1, t, D_ATTN), row),
        _const_spec((CONV_WIDTH, SUBLANES, D_CONV)),
        _const_spec((1, D_CONV)),
        _const_spec((1, D_CONV)),
        _const_spec((1, D_CONV)),
        _const_spec((1, D_ATTN)),
        _const_spec((1, D_CONV)),
        _const_spec((D_MODEL, D_MODEL)),
    ]
    return pl.pallas_call(
        functools.partial(_convout_kernel, rc=min(t, 128)), name=f"convout_b{b}",
        grid=(b, l // t), in_specs=in_specs,
        out_specs=pl.BlockSpec((1, t, D_MODEL), row),
        out_shape=jax.ShapeDtypeStruct((b, l, D_MODEL), F32),
        scratch_shapes=[pltpu.VMEM((t + CONV_HIST_PAD, D_CONV), F32), pltpu.VMEM((t, D_CONV), F32)],
        compiler_params=_cparams(("parallel", "arbitrary")),
    )(x, glu, glu, hist, ao, w["conv_w8"], w["conv_b"], w["ln_g"], w["ln_b"], w["gattn"],
      w["gconv"], w["wout"])


def _ffn_kernel(*refs, ts, has_prev, st0):
    if has_prev:
        x_ref, prev_ref, hist_ref, g_ref, wup_ref, cw_ref, cb_ref, wdn_ref, o_ref, st_ref, act_ref = refs
    else:
        x_ref, hist_ref, g_ref, wup_ref, cw_ref, cb_ref, wdn_ref, o_ref, st_ref, act_ref = refs
    i = pl.program_id(1)
    x = x_ref[0]
    t = x.shape[0]
    hr = hist_ref.shape[1]
    g = g_ref[...]
    if has_prev:
        xe = jnp.concatenate([prev_ref[0], x], axis=0)
    else:
        xe = x
    hx = (xe * _rms_scale(xe, D_MODEL) * g).astype(BF16)
    cw = cw_ref[...]
    cb = cb_ref[...]
    w2 = 2 * FFN_CHUNK
    for c in range(D_FF // FFN_CHUNK):
        cs = slice(c * w2, (c + 1) * w2)
        u = jnp.dot(hx, wup_ref[:, cs], preferred_element_type=F32)
        hist = hist_ref[0, :, cs]
        if has_prev:
            u_ext = jnp.concatenate([jnp.where(i == 0, hist, u[:hr]), u[hr:]], axis=0)
        else:
            u_ext = jnp.concatenate([hist, u], axis=0)
        st_ref[0, :, cs] = u_ext[st0:st0 + 2 * ts]
        uc = (cw[0:1, cs] * u_ext[hr - 2 * ts:hr - 2 * ts + t]
              + cw[1:2, cs] * u_ext[hr - ts:hr - ts + t]
              + cw[2:3, cs] * u_ext[hr:] + cb[:, cs])
        gate = uc[:, :FFN_CHUNK]
        act = gate * jax.nn.sigmoid(gate) * uc[:, FFN_CHUNK:]
        act_ref[:, c * FFN_CHUNK:(c + 1) * FFN_CHUNK] = act.astype(BF16)
    o_ref[0] = x + jnp.dot(act_ref[...], wdn_ref[...], preferred_element_type=F32)


def _ffn(x, hist, w, *, t, ts, has_prev, n_valid=None):
    b, l, _ = x.shape
    hr = hist.shape[1]
    st0 = hr + (t if n_valid is None else n_valid) - 2 * ts
    per_batch_hist = hist.shape[0] > 1
    row = lambda bi, i: (bi, i, 0)
    in_specs = [pl.BlockSpec((1, t, D_MODEL), row)]
    args = [x]
    if has_prev:
        hb = t // hr
        in_specs.append(pl.BlockSpec((1, hr, D_MODEL), lambda bi, i: (bi, jnp.maximum(i * hb - 1, 0), 0)))
        args.append(x)
    in_specs += [
        pl.BlockSpec((1, hr, 2 * D_FF),
                     (lambda bi, i: (bi, 0, 0)) if per_batch_hist else (lambda bi, i: (0, 0, 0))),
        _const_spec((1, D_MODEL)),
        _const_spec((D_MODEL, 2 * D_FF)),
        _const_spec((SUBLANES, 2 * D_FF)),
        _const_spec((1, 2 * D_FF)),
        _const_spec((D_FF, D_MODEL)),
    ]
    args += [hist, w["gffn"], w["wup"], w["ffn_w"], w["ffn_b"], w["wdn"]]
    out, st = pl.pallas_call(
        functools.partial(_ffn_kernel, ts=ts, has_prev=has_prev, st0=st0), name=f"ffn_b{b}_ts{ts}",
        grid=(b, l // t), in_specs=in_specs,
        out_specs=(pl.BlockSpec((1, t, D_MODEL), row),
                   pl.BlockSpec((1, 2 * ts, 2 * D_FF), lambda bi, i: (bi, 0, 0))),
        out_shape=(jax.ShapeDtypeStruct((b, l, D_MODEL), F32),
                   jax.ShapeDtypeStruct((b, 2 * ts, 2 * D_FF), F32)),
        scratch_shapes=[pltpu.VMEM((t, D_FF), BF16)],
        compiler_params=_cparams(("parallel", "arbitrary")),
    )(*args)
    return out, st


def _qlat_kernel(q_ref, gk_ref, wuk_ref, o_ref):
    gk = gk_ref[...]
    n = q_ref.shape[0]
    for hd in range(N_HEADS):
        sl = slice(hd * HEAD_LANES, (hd + 1) * HEAD_LANES)
        qg = q_ref[:, sl].astype(F32) * gk
        lat = lax.dot_general(qg.astype(BF16), wuk_ref[:, sl], (((1,), (1,)), ((), ())),
                              preferred_element_type=F32)
        lane = lax.broadcasted_iota(jnp.int32, (1, LANES), 1)
        rope = jnp.where(lane < QK_ROPE, qg, 0.0)
        o_ref[hd] = jnp.concatenate([lat, rope], axis=1).astype(BF16)


def _qlat(q, w):
    n = q.shape[0]
    return pl.pallas_call(
        _qlat_kernel, name="sample_qlat",
        out_shape=jax.ShapeDtypeStruct((N_HEADS, n, 2 * LANES), BF16),
    )(q, w["gk"], w["wuk"])


def _sattn_kernel(pt_ref, ckv_hbm, kpe_hbm, newc_ref, newp_ref, aconst_ref, qlat_ref, o_ref,
                  cbuf, pbuf, sem, a_ref, *, group, page, n_new, n_groups, nslot):
    b = pl.program_id(0)
    nb = pl.num_programs(0)
    per_round = nslot // 2
    n_rounds = n_groups // per_round
    nconst = aconst_ref.shape[0]
    rows = a_ref.shape[0] - nconst
    tb = (((1,), (1,)), ((), ()))

    def group_copies(bb, g, slot):
        cps = []
        for j in range(group):
            pg = pt_ref[bb, g * group + j]
            cps.append(pltpu.make_async_copy(
                ckv_hbm.at[0, pg], cbuf.at[slot, pl.ds(j * page, page), :], sem.at[0, slot]))
            cps.append(pltpu.make_async_copy(
                kpe_hbm.at[0, pg], pbuf.at[slot, :, pl.ds(j * page, page)], sem.at[1, slot]))
        return cps

    def round_copies(bb, rd, half):
        return [cp for u in range(per_round)
                for cp in group_copies(bb, rd * per_round + u, half * per_round + u)]

    @pl.when(b == 0)
    def _():
        a_ref[0:nconst] = aconst_ref[...]
        for cp in round_copies(0, 0, 0):
            cp.start()

    a_ref[nconst:] = qlat_ref[0, :, 0:KV_LORA]
    q_rope = qlat_ref[0, :, KV_LORA:KV_LORA + QK_ROPE]

    def expand(ckv_f32):
        half = nconst // 2
        ckv_b = ckv_f32.astype(BF16)
        r0 = lax.dot_general(a_ref[0:half], ckv_b, tb, preferred_element_type=F32)
        r1 = lax.dot_general(a_ref[half:], ckv_b, tb, preferred_element_type=F32)
        return ckv_b, r0, r1

    def chain(expanded, kpe_t, mask=None):
        ckv_b, r0, r1 = expanded
        tk = ckv_b.shape[0]
        half = nconst // 2
        kn = jnp.concatenate([r0, r1[0:half]], axis=0)
        ss = jnp.sum((kn * kn).reshape(N_HEADS, QK_NOPE, tk), axis=1)
        ss = ss + jnp.sum(kpe_t * kpe_t, axis=0, keepdims=True)
        rinv = lax.rsqrt(ss * (1.0 / QK_DIM) + EPS)
        sc = r1[half:] + jnp.dot(q_rope, kpe_t.astype(BF16), preferred_element_type=F32)
        s = sc * jnp.concatenate([rinv] * n_new, axis=0)
        if mask is not None:
            s = jnp.where(mask, s, -jnp.inf)
        m = jnp.max(s, axis=1, keepdims=True)
        p = jnp.exp2(s - m)
        return m, jnp.sum(p, axis=1, keepdims=True), jnp.dot(p.astype(BF16), ckv_b,
                                                             preferred_element_type=F32)

    def merge(state, part):
        m_old, l_old, acc_old = state
        m, l, acc = part
        m_new = jnp.maximum(m_old, m)
        w_old = jnp.exp2(m_old - m_new)
        w = jnp.exp2(m - m_new)
        return m_new, l_old * w_old + l * w, acc_old * w_old + acc * w

    def two_rounds(it, state):
        for half in range(2):
            rd = it * 2 + half
            wrap = rd + 1 >= n_rounds
            bb = jnp.minimum(jnp.where(wrap, b + 1, b), nb - 1)
            for cp in round_copies(bb, jnp.where(wrap, 0, rd + 1), 1 - half):
                cp.start()
            for cp in round_copies(b, rd, half):
                cp.wait()
            base = half * per_round
            nxt = expand(cbuf[base])
            for u in range(per_round):
                cur = nxt
                if u + 1 < per_round:
                    nxt = expand(cbuf[base + u + 1])
                state = merge(state, chain(cur, pbuf[base + u]))
        return state

    state = (jnp.full((rows, 1), -jnp.inf, F32), jnp.zeros((rows, 1), F32),
             jnp.zeros((rows, KV_LORA), F32))
    state = lax.fori_loop(0, n_rounds // 2, two_rounds, state)

    tok = lax.broadcasted_iota(jnp.int32, (rows, page), 1)
    step = lax.broadcasted_iota(jnp.int32, (rows, page), 0) // N_HEADS
    _, l, acc = merge(state, chain(expand(newc_ref[0]), newp_ref[0], mask=tok <= step))
    o_ref[0] = acc / l

    @pl.when(b == nb - 1)
    def _():
        for cp in round_copies(b, 0, 0):
            cp.wait()


def _sample_attention(page_table, cache_ckv, cache_kpe_t, newc, newp_t, aconst, qlat, *, group, nslot):
    nb, n_pages = page_table.shape
    page = cache_ckv.shape[2]
    nrow = qlat.shape[1]
    n_new = nrow // N_HEADS
    nconst = aconst.shape[0]
    n_groups = n_pages // group
    assert n_pages % group == 0 and n_groups % nslot == 0
    in_specs = [
        pl.BlockSpec(memory_space=pl.ANY),
        pl.BlockSpec(memory_space=pl.ANY),
        pl.BlockSpec((1, page, KV_LORA), lambda b, pt: (b, 0, 0)),
        pl.BlockSpec((1, QK_ROPE, page), lambda b, pt: (b, 0, 0)),
        pl.BlockSpec((nconst, KV_LORA), lambda b, pt: (0, 0)),
        pl.BlockSpec((1, nrow, 2 * LANES), lambda b, pt: (b, 0, 0)),
    ]
    grid_spec = pltpu.PrefetchScalarGridSpec(
        num_scalar_prefetch=1, grid=(nb,), in_specs=in_specs,
        out_specs=pl.BlockSpec((1, nrow, KV_LORA), lambda b, pt: (b, 0, 0)),
        scratch_shapes=[
            pltpu.VMEM((nslot, group * page, KV_LORA), F32),
            pltpu.VMEM((nslot, QK_ROPE, group * page), F32),
            pltpu.SemaphoreType.DMA((2, nslot)),
            pltpu.VMEM((nconst + nrow, KV_LORA), BF16),
        ])
    return pl.pallas_call(
        functools.partial(_sattn_kernel, group=group, page=page, n_new=n_new, n_groups=n_groups,
                          nslot=nslot),
        name="sample_attn", grid_spec=grid_spec,
        out_shape=jax.ShapeDtypeStruct((nb, nrow, KV_LORA), F32),
        compiler_params=_cparams(("arbitrary",)),
    )(page_table, cache_ckv, cache_kpe_t, newc, newp_t, aconst, qlat)


def _spost_kernel(x_ref, lat_ref, ext_ref, wuvp_ref, cw_ref, cb_ref, lng_ref, lnb_ref, gattn_ref,
                  gconv_ref, wout_ref, o_ref, y_ref, *, nb):
    n = x_ref.shape[0]
    n_new = n // nb
    ao = jnp.zeros((n, D_ATTN), F32)
    for hd in range(N_HEADS):
        ao = ao + jnp.dot(lat_ref[hd].astype(BF16), wuvp_ref[hd], preferred_element_type=F32)
    attn_n = (ao * _rms_scale(ao, D_ATTN) * gattn_ref[...]).astype(BF16)
    cw = cw_ref[...]
    for s in range(n_new):
        acc = jnp.zeros((nb, D_CONV), F32)
        for kk in range(CONV_WIDTH):
            acc = acc + cw[kk:kk + 1] * ext_ref[s + kk]
        y_ref[s * nb:(s + 1) * nb] = acc
    conv_n = _conv_tail(y_ref[...] + cb_ref[...], lng_ref[...], lnb_ref[...], gconv_ref[...])
    upd = jnp.dot(attn_n, wout_ref[0:D_ATTN], preferred_element_type=F32)
    upd = upd + jnp.dot(conv_n, wout_ref[D_ATTN:], preferred_element_type=F32)
    o_ref[...] = x_ref[...] + upd


def _sample_post(x, lat, ext, w, *, nb):
    n = x.shape[0]
    return pl.pallas_call(
        functools.partial(_spost_kernel, nb=nb), name="sample_post",
        out_shape=jax.ShapeDtypeStruct((n, D_MODEL), F32),
        scratch_shapes=[pltpu.VMEM((n, D_CONV), F32)],
        compiler_params=pltpu.CompilerParams(vmem_limit_bytes=VMEM_LIMIT),
    )(x, lat, ext, w["wuv_pad"], w["conv_w"], w["conv_b"], w["ln_g"], w["ln_b"], w["gattn"],
      w["gconv"], w["wout"])


def _head_layout(g):
    return jnp.concatenate([g[QK_NOPE:], g[:QK_NOPE], jnp.zeros((HEAD_LANES - QK_DIM,), F32)])[None]


def _prep_weights(norm_mix_g, w_in, q_a_norm_g, w_uq, kv_a_norm_g, w_ukv, q_norm_g, k_norm_g,
                  conv_dw_w, conv_dw_b, conv_ln_g, conv_ln_b, out_norm_attn_g, out_norm_conv_g,
                  w_out, norm_ffn_g, w_ffn_up, ffn_dw_w, ffn_dw_b, w_ffn_down):
    half = QK_ROPE // 2
    o = 2 * D_CONV + Q_LORA + KV_LORA
    kp1, kp2 = w_in[:, o:o + half], w_in[:, o + half:o + QK_ROPE]
    win = jnp.concatenate([w_in, -kp2, kp1, jnp.zeros((D_MODEL, LANES - 2 * QK_ROPE), F32)], axis=1)

    uq = w_uq.reshape(Q_LORA, N_HEADS, QK_DIM)
    x1, x2 = uq[..., QK_NOPE:QK_NOPE + half], uq[..., QK_NOPE + half:]
    zq = jnp.zeros((Q_LORA, N_HEADS, HEAD_LANES - QK_DIM), F32)
    q_main = jnp.concatenate([x1, x2, uq[..., :QK_NOPE], zq], axis=-1)
    q_rot = jnp.concatenate([-x2, x1, jnp.zeros((Q_LORA, N_HEADS, HEAD_LANES - QK_ROPE), F32)], axis=-1)
    wuq = jnp.concatenate([q_main.reshape(Q_LORA, D_QK_PAD), q_rot.reshape(Q_LORA, D_QK_PAD)], axis=1)

    ukv = w_ukv.reshape(KV_LORA, N_HEADS, QK_NOPE + V_DIM)
    uk, uv = ukv[..., :QK_NOPE], ukv[..., QK_NOPE:]
    wuk = jnp.concatenate([jnp.zeros((KV_LORA, N_HEADS, QK_ROPE), F32), uk,
                           jnp.zeros((KV_LORA, N_HEADS, HEAD_LANES - QK_DIM), F32)],
                          axis=-1).reshape(KV_LORA, D_QK_PAD)
    wuvT = uv.reshape(KV_LORA, D_ATTN).T
    eye = jnp.eye(N_HEADS, dtype=F32)
    wuv_pad = jnp.einsum("chd,hg->hcgd", uv, eye).reshape(N_HEADS, KV_LORA, D_ATTN)
    aconst = uk.transpose(1, 2, 0).reshape(N_HEADS * QK_NOPE, KV_LORA)

    interleave = _interleave
    scale = QK_DIM ** -0.5 * LOG2E
    return dict(
        gmix=norm_mix_g[None], win=win.astype(BF16), gqa=q_a_norm_g[None], wuq=wuq.astype(BF16),
        gkva=kv_a_norm_g[None], wuk=wuk.astype(BF16), wuvT=wuvT.astype(BF16),
        gq=_head_layout(q_norm_g) * scale, gk=_head_layout(k_norm_g),
        wuv_pad=wuv_pad.astype(BF16), aconst=aconst.astype(BF16),
        conv_w=jnp.pad(conv_dw_w, ((0, CONV_HIST_PAD - CONV_WIDTH), (0, 0))), conv_b=conv_dw_b[None],
        conv_w8=jnp.broadcast_to(conv_dw_w[:, None, :], (CONV_WIDTH, SUBLANES, D_CONV)),
        ln_g=conv_ln_g[None], ln_b=conv_ln_b[None], gattn=out_norm_attn_g[None],
        gconv=out_norm_conv_g[None], wout=w_out.astype(BF16), gffn=norm_ffn_g[None],
        wup=interleave(w_ffn_up.astype(BF16)),
        ffn_w=jnp.pad(interleave(ffn_dw_w), ((0, SUBLANES - FFN_CONV_WIDTH), (0, 0))),
        ffn_b=interleave(ffn_dw_b)[None], wdn=w_ffn_down.astype(BF16),
    )


def _interleave(a):
    parts = []
    for c in range(D_FF // FFN_CHUNK):
        parts.append(a[..., c * FFN_CHUNK:(c + 1) * FFN_CHUNK])
        parts.append(a[..., D_FF + c * FFN_CHUNK:D_FF + (c + 1) * FFN_CHUNK])
    return jnp.concatenate(parts, axis=-1)


def _deinterleave(a):
    w2 = 2 * FFN_CHUNK
    nchunk = D_FF // FFN_CHUNK
    gate = [a[..., c * w2:c * w2 + FFN_CHUNK] for c in range(nchunk)]
    up = [a[..., c * w2 + FFN_CHUNK:(c + 1) * w2] for c in range(nchunk)]
    return jnp.concatenate(gate + up, axis=-1)


def _rope_lanes(pos):
    inv_freq = ROPE_THETA ** (-jnp.arange(0, QK_ROPE, 2, dtype=F32) / QK_ROPE)
    ang = pos.astype(F32)[:, None] * inv_freq[None, :]
    pad = jnp.zeros((pos.shape[0], LANES - QK_ROPE), F32)
    cos = jnp.concatenate([jnp.cos(ang), jnp.cos(ang), pad], axis=1)
    sin = jnp.concatenate([jnp.sin(ang), jnp.sin(ang), pad], axis=1)
    return cos, sin


def _prompt_layer(x, pos, conv_hist, ffn_hist, k_prefix, vT_prefix, w, *, t, tq, tr, tf, n_valid=None):
    cos, sin = _rope_lanes(pos)
    glu, q, k, vT, ckv, kpe = _proj(x, cos, sin, w, t=t, tq=tq)
    ao = _attention(q, k, vT, k_prefix, vT_prefix, tq=tq)
    x1 = _convout(x, glu, conv_hist, ao, w, t=tr)
    y, ffn_state = _ffn(x1, ffn_hist, w, t=tf, ts=1, has_prev=True, n_valid=n_valid)
    return y, glu, k, vT, ckv, kpe, ffn_state


def kernel(x_prompt, x_sample, cache_ckv, cache_kpe, state_conv, state_ffn, page_table, meta_tokens,
           norm_mix_g, w_in, q_a_norm_g, w_uq, kv_a_norm_g, w_ukv, q_norm_g, k_norm_g, conv_dw_w,
           conv_dw_b, conv_ln_g, conv_ln_b, out_norm_attn_g, out_norm_conv_g, w_out, norm_ffn_g,
           w_ffn_up, ffn_dw_w, ffn_dw_b, w_ffn_down):
    assert norm_mix_g.shape[0] == 1, "single-layer step"
    w = _prep_weights(norm_mix_g[0], w_in[0], q_a_norm_g[0], w_uq[0], kv_a_norm_g[0], w_ukv[0],
                      q_norm_g[0], k_norm_g[0], conv_dw_w[0], conv_dw_b[0], conv_ln_g[0],
                      conv_ln_b[0], out_norm_attn_g[0], out_norm_conv_g[0], w_out[0], norm_ffn_g[0],
                      w_ffn_up[0], ffn_dw_w[0], ffn_dw_b[0], w_ffn_down[0])
    bp, seq, _ = x_prompt.shape
    tm = 256
    tp = 512
    tr = 256
    assert seq % tp == 0

    xm = jnp.pad(meta_tokens.astype(F32), ((0, tm - N_META), (0, 0)))[None]
    zc = jnp.zeros((1, CONV_HIST_PAD, D_CONV), F32)
    zf = jnp.zeros((1, FFN_HIST_PAD, 2 * D_FF), F32)
    _, glu_m, k_m, vT_m, ckv_m, kpe_m, st_m = _prompt_layer(
        xm, jnp.arange(tm), zc, zf, None, None, w, t=tm, tq=tm, tr=tm, tf=tm, n_valid=N_META)

    conv_hist = jnp.pad(glu_m[:, :N_META], ((0, 0), (CONV_HIST_PAD - N_META, 0), (0, 0)))
    ffn_hist = jnp.pad(st_m, ((0, 0), (FFN_HIST_PAD - (FFN_CONV_WIDTH - 1), 0), (0, 0)))
    y_prompt, glu_p, _, _, ckv_p, kpe_p, st_p = _prompt_layer(
        x_prompt, N_META + jnp.arange(seq), conv_hist, ffn_hist,
        k_m[:, :N_META], vT_m[:, 0, :, :N_META], w, t=tp, tq=tp, tr=tr, tf=tp)
    ckv_prompt = jnp.concatenate(
        [jnp.broadcast_to(ckv_m[:, :N_META], (bp, N_META, KV_LORA)), ckv_p], axis=1)[None]
    kpe_prompt = jnp.concatenate(
        [jnp.broadcast_to(kpe_m[:, :N_META], (bp, N_META, QK_ROPE)), kpe_p], axis=1)[None]
    conv_prompt = glu_p[:, seq - (CONV_WIDTH - 1):][None]
    ffn_prompt = _deinterleave(st_p)[None]

    nb, ns, _ = x_sample.shape
    n = nb * ns
    n_hist = FFN_CONV_WIDTH - 1
    assert ns >= n_hist and ns <= SUBLANES
    page = cache_ckv.shape[2]
    past_len = page_table.shape[1] * page
    xs = x_sample.transpose(1, 0, 2).reshape(1, n, D_MODEL)
    cos_s, sin_s = _rope_lanes(past_len + jnp.arange(n) // nb)
    glu_s, q_s, _, _, ckv_s, kpe_s = _proj(xs, cos_s, sin_s, w, t=n, tq=n)

    def to_seq_major(a):
        return a.reshape(ns, nb, a.shape[-1]).transpose(1, 0, 2)

    qlat = _qlat(q_s[0], w)
    qlat = qlat.reshape(N_HEADS, ns, nb, 2 * LANES).transpose(2, 1, 0, 3).reshape(nb, ns * N_HEADS, 2 * LANES)
    ckv_sm, kpe_sm = to_seq_major(ckv_s[0]), to_seq_major(kpe_s[0])
    newc = jnp.pad(ckv_sm, ((0, 0), (0, page - ns), (0, 0)))
    newp_t = jnp.pad(kpe_sm.transpose(0, 2, 1), ((0, 0), (0, 0), (0, page - ns)))
    lat = _sample_attention(page_table, cache_ckv, jnp.swapaxes(cache_kpe, 2, 3), newc, newp_t,
                            w["aconst"], qlat, group=SAMPLE_GROUP, nslot=SAMPLE_SLOTS)
    lat = lat.reshape(nb, ns, N_HEADS, KV_LORA).transpose(2, 1, 0, 3).reshape(N_HEADS, n, KV_LORA)
    ext = jnp.concatenate([state_conv[0].transpose(1, 0, 2), glu_s[0].reshape(ns, nb, D_CONV)], axis=0)
    x1_s = _sample_post(xs[0], lat, ext, w, nb=nb)
    hist_s = _interleave(state_ffn[0]).transpose(1, 0, 2).reshape(1, n_hist * nb, 2 * D_FF)
    y_s, st_s = _ffn(x1_s[None], hist_s, w, t=n, ts=nb, has_prev=False)

    y_sample = to_seq_major(y_s[0])
    conv_sample = jnp.concatenate([state_conv[0][:, ns:], to_seq_major(glu_s[0])], axis=1)[None]
    ffn_sample = _deinterleave(st_s[0]).reshape(n_hist, nb, 2 * D_FF).transpose(1, 0, 2)[None]
    return (y_prompt, y_sample, ckv_prompt, kpe_prompt, conv_prompt, ffn_prompt,
            ckv_sm[None], kpe_sm[None], conv_sample, ffn_sample)
```

```python
import functools

import jax
import jax.numpy as jnp
from jax import lax
from jax.experimental import pallas as pl
from jax.experimental.pallas import tpu as pltpu

F32 = jnp.float32
BF16 = jnp.bfloat16

D_MODEL = 1024
N_META = 16
D_CONV = 512
CONV_WIDTH = 31
N_HEADS = 8
QK_NOPE = 64
QK_ROPE = 32
QK_DIM = QK_NOPE + QK_ROPE
V_DIM = 64
D_ATTN = N_HEADS * V_DIM
Q_LORA = 256
KV_LORA = 128
ROPE_THETA = 10000.0
D_FF = 2816
FFN_CONV_WIDTH = 3
EPS = 1e-6

LANES = 128
SUBLANES = 8
HEAD_LANES = LANES
D_QK_PAD = N_HEADS * HEAD_LANES
D_IN_PAD = 2 * D_CONV + Q_LORA + KV_LORA + LANES
CONV_HIST_PAD = 32
FFN_HIST_PAD = 16
FFN_CHUNK = 256
V_ROWS = 80
LOG2E = 1.4426950408889634
ATTN_HEADS_PER_STEP = 4
SAMPLE_GROUP = 32
SAMPLE_SLOTS = 4
VMEM_LIMIT = 56 * 1024 * 1024


def _cparams(sem):
    return pltpu.CompilerParams(dimension_semantics=sem, vmem_limit_bytes=VMEM_LIMIT)


def _rms_scale(x, n):
    return lax.rsqrt(jnp.sum(x * x, axis=-1, keepdims=True) * (1.0 / n) + EPS)


def _const_spec(shape):
    nd = len(shape)
    return pl.BlockSpec(shape, lambda *_: (0,) * nd)


def _proj_kernel(x_ref, cos_ref, sin_ref, gmix_ref, win_ref, gqa_ref, wuq_ref, gkva_ref, wuk_ref,
                 wuvT_ref, gq_ref, gk_ref, glu_ref, q_ref, k_ref, vT_ref, ckv_ref, kpe_ref, *, tq, th):
    t = x_ref.shape[1]
    lane = lax.broadcasted_iota(jnp.int32, (1, LANES), 1)
    nope = jnp.where((lane >= QK_ROPE) & (lane < QK_DIM), 1.0, 0.0).astype(F32)
    gq = gq_ref[...]
    gk = gk_ref[...]
    pad_row = lax.broadcasted_iota(jnp.int32, (V_ROWS - V_DIM, th), 0)
    ones_rows = jnp.where(pad_row == 0, 1.0, 0.0).astype(BF16)

    def in_proj(r0):
        x = x_ref[0, r0:r0 + th]
        h = (x * _rms_scale(x, D_MODEL) * gmix_ref[...]).astype(BF16)
        return jnp.dot(h, win_ref[...], preferred_element_type=F32)

    def low_rank(r0, proj):
        rows = slice(r0, r0 + th)
        glu_ref[0, rows] = proj[:, :D_CONV] * jax.nn.sigmoid(proj[:, D_CONV:2 * D_CONV])
        o = 2 * D_CONV
        cq = proj[:, o:o + Q_LORA]
        ckv_raw = proj[:, o + Q_LORA:o + Q_LORA + KV_LORA]
        kp = proj[:, o + Q_LORA + KV_LORA:]
        cqn = (cq * _rms_scale(cq, Q_LORA) * gqa_ref[...]).astype(BF16)
        qq = jnp.dot(cqn, wuq_ref[...], preferred_element_type=F32)
        ckv = ckv_raw * _rms_scale(ckv_raw, KV_LORA) * gkva_ref[...]
        ckv_ref[0, rows] = ckv
        cos = cos_ref[rows]
        sin = sin_ref[rows]
        kpe = kp * cos + pltpu.roll(kp, LANES - QK_ROPE, 1) * sin
        kpe_ref[0, rows] = kpe[:, :QK_ROPE]
        ckv_b = ckv.astype(BF16)
        kn = jnp.dot(ckv_b, wuk_ref[...], preferred_element_type=F32)
        vT = lax.dot_general(wuvT_ref[...], ckv_b, (((1,), (1,)), ((), ())),
                             preferred_element_type=F32).astype(BF16)
        j, c0 = divmod(r0, tq)
        for hd in range(N_HEADS):
            vT_ref[0, j, hd * V_ROWS:hd * V_ROWS + V_DIM, c0:c0 + th] = vT[hd * V_DIM:(hd + 1) * V_DIM]
            vT_ref[0, j, hd * V_ROWS + V_DIM:(hd + 1) * V_ROWS, c0:c0 + th] = ones_rows
        return qq, kn, kpe, cos, sin

    def heads_out(r0, qq, kn, kpe, cos, sin):
        rows = slice(r0, r0 + th)
        cosq = cos + nope
        for hd in range(N_HEADS):
            sl = slice(hd * HEAD_LANES, (hd + 1) * HEAD_LANES)
            qh = qq[:, sl] * cosq + qq[:, D_QK_PAD + hd * HEAD_LANES:D_QK_PAD + (hd + 1) * HEAD_LANES] * sin
            q_ref[0, rows, sl] = (qh * _rms_scale(qh, QK_DIM) * gq).astype(BF16)
            kh = kn[:, sl] + kpe
            k_ref[0, rows, sl] = (kh * _rms_scale(kh, QK_DIM) * gk).astype(BF16)

    starts = range(0, t, th)
    projs = [in_proj(r0) for r0 in starts]
    mids = [low_rank(r0, p) for r0, p in zip(starts, projs)]
    for r0, mid in zip(starts, mids):
        heads_out(r0, *mid)


def _proj(x, cos, sin, w, *, t, tq):
    b, l, _ = x.shape
    th = t // 2 if (t // 2) % LANES == 0 and tq % (t // 2) == 0 else t
    kern = functools.partial(_proj_kernel, tq=tq, th=th)
    out_shape = (
        jax.ShapeDtypeStruct((b, l, D_CONV), F32),
        jax.ShapeDtypeStruct((b, l, D_QK_PAD), BF16),
        jax.ShapeDtypeStruct((b, l, D_QK_PAD), BF16),
        jax.ShapeDtypeStruct((b, l // tq, N_HEADS * V_ROWS, tq), BF16),
        jax.ShapeDtypeStruct((b, l, KV_LORA), F32),
        jax.ShapeDtypeStruct((b, l, QK_ROPE), F32),
    )
    row = lambda bi, i: (bi, i, 0)
    in_specs = [
        pl.BlockSpec((1, t, D_MODEL), row),
        pl.BlockSpec((t, LANES), lambda bi, i: (i, 0)),
        pl.BlockSpec((t, LANES), lambda bi, i: (i, 0)),
        _const_spec((1, D_MODEL)),
        _const_spec((D_MODEL, D_IN_PAD)),
        _const_spec((1, Q_LORA)),
        _const_spec((Q_LORA, 2 * D_QK_PAD)),
        _const_spec((1, KV_LORA)),
        _const_spec((KV_LORA, D_QK_PAD)),
        _const_spec((D_ATTN, KV_LORA)),
        _const_spec((1, LANES)),
        _const_spec((1, LANES)),
    ]
    out_specs = (
        pl.BlockSpec((1, t, D_CONV), row),
        pl.BlockSpec((1, t, D_QK_PAD), row),
        pl.BlockSpec((1, t, D_QK_PAD), row),
        pl.BlockSpec((1, t // tq, N_HEADS * V_ROWS, tq), lambda bi, i: (bi, i, 0, 0)),
        pl.BlockSpec((1, t, KV_LORA), row),
        pl.BlockSpec((1, t, QK_ROPE), row),
    )
    return pl.pallas_call(
        kern, name=f"proj_b{b}", grid=(b, l // t), in_specs=in_specs, out_specs=out_specs,
        out_shape=out_shape,
        compiler_params=_cparams(("parallel", "parallel")),
    )(x, cos, sin, w["gmix"], w["win"], w["gqa"], w["wuq"], w["gkva"], w["wuk"], w["wuvT"],
      w["gq"], w["gk"])


def _softmax_update(m, acc, s, vT):
    m_new = jnp.maximum(m, jnp.max(s, axis=0, keepdims=True))
    p = jnp.exp2(s - m_new).astype(BF16)
    acc = acc * jnp.exp2(m - m_new) + jnp.dot(vT, p, preferred_element_type=F32)
    return m_new, acc


def _attn_kernel(*refs, tq, nq, n_prefix, nh):
    if n_prefix:
        q_ref, k_ref, vT_ref, kp_ref, vTp_ref, o_ref, oT_ref, s0_ref = refs
    else:
        q_ref, k_ref, vT_ref, o_ref, oT_ref, s0_ref = refs
    tb = (((1,), (1,)), ((), ()))
    row_i = lax.broadcasted_iota(jnp.int32, (tq, tq), 0)
    col_i = lax.broadcasted_iota(jnp.int32, (tq, tq), 1)
    causal = row_i <= col_i
    heads = range(nh)
    lanes = [slice(h * HEAD_LANES, (h + 1) * HEAD_LANES) for h in heads]
    vrows = [slice(h * V_ROWS, (h + 1) * V_ROWS) for h in heads]

    def q_body(qi, carry):
        q0 = pl.multiple_of(qi * tq, tq)
        q_t = [q_ref[0, pl.ds(q0, tq), lanes[h]] for h in heads]
        state = []
        for h in heads:
            if n_prefix:
                s0 = lax.dot_general(kp_ref[0, :, lanes[h]], q_t[h], tb, preferred_element_type=F32)
                m = jnp.max(s0, axis=0, keepdims=True)
                acc = jnp.dot(vTp_ref[0, vrows[h], :], jnp.exp2(s0 - m).astype(BF16),
                              preferred_element_type=F32)
            else:
                m = jnp.full((1, tq), -jnp.inf, F32)
                acc = jnp.zeros((V_ROWS, tq), F32)
            state += [m, acc]

        def scores(k0, h):
            return lax.dot_general(k_ref[0, pl.ds(k0, tq), lanes[h]], q_t[h], tb,
                                   preferred_element_type=F32)

        s0_ref[...] = scores(0, 0)

        def k_body(kj, c):
            k0 = pl.multiple_of(kj * tq, tq)
            out = []
            s_next = s0_ref[...]
            for h in heads:
                s = s_next
                if h + 1 < nh:
                    s_next = scores(k0, h + 1)
                else:
                    s0_ref[...] = scores(k0 + tq, 0)
                out += _softmax_update(c[2 * h], c[2 * h + 1], s, vT_ref[0, kj, vrows[h], :])
            return tuple(out)

        state = lax.fori_loop(0, qi, k_body, tuple(state))
        s_next = s0_ref[...]
        for h in heads:
            s = jnp.where(causal, s_next, -jnp.inf)
            if h + 1 < nh:
                s_next = scores(q0, h + 1)
            _, acc = _softmax_update(state[2 * h], state[2 * h + 1], s, vT_ref[0, qi, vrows[h], :])
            oT_ref[qi, h * V_DIM:(h + 1) * V_DIM, :] = acc[:V_DIM] / acc[V_DIM:V_DIM + 1]
        return carry

    lax.fori_loop(0, nq, q_body, 0)
    for qi in range(nq):
        o_ref[0, qi * tq:(qi + 1) * tq, :] = oT_ref[qi].T


def _attention(q, k, vT, k_prefix=None, vT_prefix=None, *, tq):
    b, l, _ = q.shape
    nq = l // tq
    n_prefix = 0 if k_prefix is None else k_prefix.shape[1]
    nh = ATTN_HEADS_PER_STEP
    kern = functools.partial(_attn_kernel, tq=tq, nq=nq, n_prefix=n_prefix, nh=nh)
    in_specs = [
        pl.BlockSpec((1, l, nh * HEAD_LANES), lambda bi, g: (bi, 0, g)),
        pl.BlockSpec((1, l, nh * HEAD_LANES), lambda bi, g: (bi, 0, g)),
        pl.BlockSpec((1, nq, nh * V_ROWS, tq), lambda bi, g: (bi, 0, g, 0)),
    ]
    args = [q, k, vT]
    if n_prefix:
        in_specs += [
            pl.BlockSpec((1, n_prefix, nh * HEAD_LANES), lambda bi, g: (0, 0, g)),
            pl.BlockSpec((1, nh * V_ROWS, n_prefix), lambda bi, g: (0, g, 0)),
        ]
        args += [k_prefix, vT_prefix]
    return pl.pallas_call(
        kern, name=f"attn_b{b}", grid=(b, N_HEADS // nh), in_specs=in_specs,
        out_specs=pl.BlockSpec((1, l, nh * V_DIM), lambda bi, g: (bi, 0, g)),
        out_shape=jax.ShapeDtypeStruct((b, l, D_ATTN), F32),
        scratch_shapes=[pltpu.VMEM((nq, nh * V_DIM, tq), F32), pltpu.VMEM((tq, tq), F32)],
        compiler_params=_cparams(("parallel", "parallel")),
    )(*args)


def _conv_tail(y, lng, lnb, gconv):
    mu = jnp.mean(y, axis=-1, keepdims=True)
    yc = y - mu
    var = jnp.mean(yc * yc, axis=-1, keepdims=True)
    z = yc * lax.rsqrt(var + EPS) * lng + lnb
    z = z * jax.nn.sigmoid(z)
    return (z * _rms_scale(z, D_CONV) * gconv).astype(BF16)


def _convout_kernel(x_ref, glu_ref, prev_ref, hist_ref, ao_ref, cw_ref, cb_ref, lng_ref, lnb_ref,
                    gattn_ref, gconv_ref, wout_ref, o_ref, ext_ref, y_ref, *, rc):
    i = pl.program_id(1)
    t = glu_ref.shape[1]
    ext_ref[0:CONV_HIST_PAD] = jnp.where(i == 0, hist_ref[0], prev_ref[0])
    ext_ref[CONV_HIST_PAD:] = glu_ref[0]
    base = CONV_HIST_PAD - (CONV_WIDTH - 1)
    nt = rc // SUBLANES
    for cc in range(D_CONV // LANES):
        cs = slice(cc * LANES, (cc + 1) * LANES)
        for r0 in range(0, t, rc):
            acc = jnp.zeros((nt, SUBLANES, LANES), F32)
            for res in range(SUBLANES):
                taps = [kk for kk in range(CONV_WIDTH) if (base + kk) % SUBLANES == res]
                shifted = ext_ref[r0 + res:r0 + base + taps[-1] + rc, cs].reshape(-1, SUBLANES, LANES)
                for kk in taps:
                    wk = cw_ref[kk, :, cs][None]
                    off = (base + kk - res) // SUBLANES
                    acc = acc + wk * shifted[off:off + nt]
            y_ref[r0:r0 + rc, cs] = acc.reshape(rc, LANES)
    conv_n = _conv_tail(y_ref[...] + cb_ref[...], lng_ref[...], lnb_ref[...], gconv_ref[...])
    ao = ao_ref[0]
    attn_n = (ao * _rms_scale(ao, D_ATTN) * gattn_ref[...]).astype(BF16)
    upd = jnp.dot(attn_n, wout_ref[0:D_ATTN], preferred_element_type=F32)
    upd = upd + jnp.dot(conv_n, wout_ref[D_ATTN:], preferred_element_type=F32)
    o_ref[0] = x_ref[0] + upd


def _convout(x, glu, hist, ao, w, *, t):
    b, l, _ = x.shape
    per_batch_hist = hist.shape[0] > 1
    hb = t // CONV_HIST_PAD
    row = lambda bi, i: (bi, i, 0)
    in_specs = [
        pl.BlockSpec((1, t, D_MODEL), row),
        pl.BlockSpec((1, t, D_CONV), row),
        pl.BlockSpec((1, CONV_HIST_PAD, D_CONV), lambda bi, i: (bi, jnp.maximum(i * hb - 1, 0), 0)),
        pl.BlockSpec((1, CONV_HIST_PAD, D_CONV),
                     (lambda bi, i: (bi, 0, 0)) if per_batch_hist else (lambda bi, i: (0, 0, 0))),
        pl.BlockSpec((1, t, D_ATTN), row),
        _const_spec((CONV_WIDTH, SUBLANES, D_CONV)),
        _const_spec((1, D_CONV)),
        _const_spec((1, D_CONV)),
        _const_spec((1, D_CONV)),
        _const_spec((1, D_ATTN)),
        _const_spec((1, D_CONV)),
        _const_spec((D_MODEL, D_MODEL)),
    ]
    return pl.pallas_call(
        functools.partial(_convout_kernel, rc=min(t, 128)), name=f"convout_b{b}",
        grid=(b, l // t), in_specs=in_specs,
        out_specs=pl.BlockSpec((1, t, D_MODEL), row),
        out_shape=jax.ShapeDtypeStruct((b, l, D_MODEL), F32),
        scratch_shapes=[pltpu.VMEM((t + CONV_HIST_PAD, D_CONV), F32), pltpu.VMEM((t, D_CONV), F32)],
        compiler_params=_cparams(("parallel", "arbitrary")),
    )(x, glu, glu, hist, ao, w["conv_w8"], w["conv_b"], w["ln_g"], w["ln_b"], w["gattn"],
      w["gconv"], w["wout"])


def _ffn_kernel(*refs, ts, has_prev, st0):
    if has_prev:
        x_ref, prev_ref, hist_ref, g_ref, wup_ref, cw_ref, cb_ref, wdn_ref, o_ref, st_ref, act_ref = refs
    else:
        x_ref, hist_ref, g_ref, wup_ref, cw_ref, cb_ref, wdn_ref, o_ref, st_ref, act_ref = refs
    i = pl.program_id(1)
    x = x_ref[0]
    t = x.shape[0]
    hr = hist_ref.shape[1]
    g = g_ref[...]
    if has_prev:
        xe = jnp.concatenate([prev_ref[0], x], axis=0)
    else:
        xe = x
    hx = (xe * _rms_scale(xe, D_MODEL) * g).astype(BF16)
    cw = cw_ref[...]
    cb = cb_ref[...]
    w2 = 2 * FFN_CHUNK
    for c in range(D_FF // FFN_CHUNK):
        cs = slice(c * w2, (c + 1) * w2)
        u = jnp.dot(hx, wup_ref[:, cs], preferred_element_type=F32)
        hist = hist_ref[0, :, cs]
        if has_prev:
            u_ext = jnp.concatenate([jnp.where(i == 0, hist, u[:hr]), u[hr:]], axis=0)
        else:
            u_ext = jnp.concatenate([hist, u], axis=0)
        st_ref[0, :, cs] = u_ext[st0:st0 + 2 * ts]
        uc = (cw[0:1, cs] * u_ext[hr - 2 * ts:hr - 2 * ts + t]
              + cw[1:2, cs] * u_ext[hr - ts:hr - ts + t]
              + cw[2:3, cs] * u_ext[hr:] + cb[:, cs])
        gate = uc[:, :FFN_CHUNK]
        act = gate * jax.nn.sigmoid(gate) * uc[:, FFN_CHUNK:]
        act_ref[:, c * FFN_CHUNK:(c + 1) * FFN_CHUNK] = act.astype(BF16)
    o_ref[0] = x + jnp.dot(act_ref[...], wdn_ref[...], preferred_element_type=F32)


def _ffn(x, hist, w, *, t, ts, has_prev, n_valid=None):
    b, l, _ = x.shape
    hr = hist.shape[1]
    st0 = hr + (t if n_valid is None else n_valid) - 2 * ts
    per_batch_hist = hist.shape[0] > 1
    row = lambda bi, i: (bi, i, 0)
    in_specs = [pl.BlockSpec((1, t, D_MODEL), row)]
    args = [x]
    if has_prev:
        hb = t // hr
        in_specs.append(pl.BlockSpec((1, hr, D_MODEL), lambda bi, i: (bi, jnp.maximum(i * hb - 1, 0), 0)))
        args.append(x)
    in_specs += [
        pl.BlockSpec((1, hr, 2 * D_FF),
                     (lambda bi, i: (bi, 0, 0)) if per_batch_hist else (lambda bi, i: (0, 0, 0))),
        _const_spec((1, D_MODEL)),
        _const_spec((D_MODEL, 2 * D_FF)),
        _const_spec((SUBLANES, 2 * D_FF)),
        _const_spec((1, 2 * D_FF)),
        _const_spec((D_FF, D_MODEL)),
    ]
    args += [hist, w["gffn"], w["wup"], w["ffn_w"], w["ffn_b"], w["wdn"]]
    out, st = pl.pallas_call(
        functools.partial(_ffn_kernel, ts=ts, has_prev=has_prev, st0=st0), name=f"ffn_b{b}_ts{ts}",
        grid=(b, l // t), in_specs=in_specs,
        out_specs=(pl.BlockSpec((1, t, D_MODEL), row),
                   pl.BlockSpec((1, 2 * ts, 2 * D_FF), lambda bi, i: (bi, 0, 0))),
        out_shape=(jax.ShapeDtypeStruct((b, l, D_MODEL), F32),
                   jax.ShapeDtypeStruct((b, 2 * ts, 2 * D_FF), F32)),
        scratch_shapes=[pltpu.VMEM((t, D_FF), BF16)],
        compiler_params=_cparams(("parallel", "arbitrary")),
    )(*args)
    return out, st


def _qlat_kernel(q_ref, gk_ref, wuk_ref, o_ref):
    gk = gk_ref[...]
    n = q_ref.shape[0]
    for hd in range(N_HEADS):
        sl = slice(hd * HEAD_LANES, (hd + 1) * HEAD_LANES)
        qg = q_ref[:, sl].astype(F32) * gk
        lat = lax.dot_general(qg.astype(BF16), wuk_ref[:, sl], (((1,), (1,)), ((), ())),
                              preferred_element_type=F32)
        lane = lax.broadcasted_iota(jnp.int32, (1, LANES), 1)
        rope = jnp.where(lane < QK_ROPE, qg, 0.0)
        o_ref[hd] = jnp.concatenate([lat, rope], axis=1).astype(BF16)


def _qlat(q, w):
    n = q.shape[0]
    return pl.pallas_call(
        _qlat_kernel, name="sample_qlat",
        out_shape=jax.ShapeDtypeStruct((N_HEADS, n, 2 * LANES), BF16),
    )(q, w["gk"], w["wuk"])


def _sattn_kernel(pt_ref, ckv_hbm, kpe_hbm, newc_ref, newp_ref, aconst_ref, qlat_ref, o_ref,
                  cbuf, pbuf, sem, a_ref, *, group, page, n_new, n_groups, nslot):
    b = pl.program_id(0)
    nb = pl.num_programs(0)
    per_round = nslot // 2
    n_rounds = n_groups // per_round
    nconst = aconst_ref.shape[0]
    rows = a_ref.shape[0] - nconst
    tb = (((1,), (1,)), ((), ()))

    def group_copies(bb, g, slot):
        cps = []
        for j in range(group):
            pg = pt_ref[bb, g * group + j]
            cps.append(pltpu.make_async_copy(
                ckv_hbm.at[0, pg], cbuf.at[slot, pl.ds(j * page, page), :], sem.at[0, slot]))
            cps.append(pltpu.make_async_copy(
                kpe_hbm.at[0, pg], pbuf.at[slot, :, pl.ds(j * page, page)], sem.at[1, slot]))
        return cps

    def round_copies(bb, rd, half):
        return [cp for u in range(per_round)
                for cp in group_copies(bb, rd * per_round + u, half * per_round + u)]

    @pl.when(b == 0)
    def _():
        a_ref[0:nconst] = aconst_ref[...]
        for cp in round_copies(0, 0, 0):
            cp.start()

    a_ref[nconst:] = qlat_ref[0, :, 0:KV_LORA]
    q_rope = qlat_ref[0, :, KV_LORA:KV_LORA + QK_ROPE]

    def expand(ckv_f32):
        half = nconst // 2
        ckv_b = ckv_f32.astype(BF16)
        r0 = lax.dot_general(a_ref[0:half], ckv_b, tb, preferred_element_type=F32)
        r1 = lax.dot_general(a_ref[half:], ckv_b, tb, preferred_element_type=F32)
        return ckv_b, r0, r1

    def chain(expanded, kpe_t, mask=None):
        ckv_b, r0, r1 = expanded
        tk = ckv_b.shape[0]
        half = nconst // 2
        kn = jnp.concatenate([r0, r1[0:half]], axis=0)
        ss = jnp.sum((kn * kn).reshape(N_HEADS, QK_NOPE, tk), axis=1)
        ss = ss + jnp.sum(kpe_t * kpe_t, axis=0, keepdims=True)
        rinv = lax.rsqrt(ss * (1.0 / QK_DIM) + EPS)
        sc = r1[half:] + jnp.dot(q_rope, kpe_t.astype(BF16), preferred_element_type=F32)
        s = sc * jnp.concatenate([rinv] * n_new, axis=0)
        if mask is not None:
            s = jnp.where(mask, s, -jnp.inf)
        m = jnp.max(s, axis=1, keepdims=True)
        p = jnp.exp2(s - m)
        return m, jnp.sum(p, axis=1, keepdims=True), jnp.dot(p.astype(BF16), ckv_b,
                                                             preferred_element_type=F32)

    def merge(state, part):
        m_old, l_old, acc_old = state
        m, l, acc = part
        m_new = jnp.maximum(m_old, m)
        w_old = jnp.exp2(m_old - m_new)
        w = jnp.exp2(m - m_new)
        return m_new, l_old * w_old + l * w, acc_old * w_old + acc * w

    def two_rounds(it, state):
        for half in range(2):
            rd = it * 2 + half
            wrap = rd + 1 >= n_rounds
            bb = jnp.minimum(jnp.where(wrap, b + 1, b), nb - 1)
            for cp in round_copies(bb, jnp.where(wrap, 0, rd + 1), 1 - half):
                cp.start()
            for cp in round_copies(b, rd, half):
                cp.wait()
            base = half * per_round
            nxt = expand(cbuf[base])
            for u in range(per_round):
                cur = nxt
                if u + 1 < per_round:
                    nxt = expand(cbuf[base + u + 1])
                state = merge(state, chain(cur, pbuf[base + u]))
        return state

    state = (jnp.full((rows, 1), -jnp.inf, F32), jnp.zeros((rows, 1), F32),
             jnp.zeros((rows, KV_LORA), F32))
    state = lax.fori_loop(0, n_rounds // 2, two_rounds, state)

    tok = lax.broadcasted_iota(jnp.int32, (rows, page), 1)
    step = lax.broadcasted_iota(jnp.int32, (rows, page), 0) // N_HEADS
    _, l, acc = merge(state, chain(expand(newc_ref[0]), newp_ref[0], mask=tok <= step))
    o_ref[0] = acc / l

    @pl.when(b == nb - 1)
    def _():
        for cp in round_copies(b, 0, 0):
            cp.wait()


def _sample_attention(page_table, cache_ckv, cache_kpe_t, newc, newp_t, aconst, qlat, *, group, nslot):
    nb, n_pages = page_table.shape
    page = cache_ckv.shape[2]
    nrow = qlat.shape[1]
    n_new = nrow // N_HEADS
    nconst = aconst.shape[0]
    n_groups = n_pages // group
    assert n_pages % group == 0 and n_groups % nslot == 0
    in_specs = [
        pl.BlockSpec(memory_space=pl.ANY),
        pl.BlockSpec(memory_space=pl.ANY),
        pl.BlockSpec((1, page, KV_LORA), lambda b, pt: (b, 0, 0)),
        pl.BlockSpec((1, QK_ROPE, page), lambda b, pt: (b, 0, 0)),
        pl.BlockSpec((nconst, KV_LORA), lambda b, pt: (0, 0)),
        pl.BlockSpec((1, nrow, 2 * LANES), lambda b, pt: (b, 0, 0)),
    ]
    grid_spec = pltpu.PrefetchScalarGridSpec(
        num_scalar_prefetch=1, grid=(nb,), in_specs=in_specs,
        out_specs=pl.BlockSpec((1, nrow, KV_LORA), lambda b, pt: (b, 0, 0)),
        scratch_shapes=[
            pltpu.VMEM((nslot, group * page, KV_LORA), F32),
            pltpu.VMEM((nslot, QK_ROPE, group * page), F32),
            pltpu.SemaphoreType.DMA((2, nslot)),
            pltpu.VMEM((nconst + nrow, KV_LORA), BF16),
        ])
    return pl.pallas_call(
        functools.partial(_sattn_kernel, group=group, page=page, n_new=n_new, n_groups=n_groups,
                          nslot=nslot),
        name="sample_attn", grid_spec=grid_spec,
        out_shape=jax.ShapeDtypeStruct((nb, nrow, KV_LORA), F32),
        compiler_params=_cparams(("arbitrary",)),
    )(page_table, cache_ckv, cache_kpe_t, newc, newp_t, aconst, qlat)


def _spost_kernel(x_ref, lat_ref, ext_ref, wuvp_ref, cw_ref, cb_ref, lng_ref, lnb_ref, gattn_ref,
                  gconv_ref, wout_ref, o_ref, y_ref, *, nb):
    n = x_ref.shape[0]
    n_new = n // nb
    ao = jnp.zeros((n, D_ATTN), F32)
    for hd in range(N_HEADS):
        ao = ao + jnp.dot(lat_ref[hd].astype(BF16), wuvp_ref[hd], preferred_element_type=F32)
    attn_n = (ao * _rms_scale(ao, D_ATTN) * gattn_ref[...]).astype(BF16)
    cw = cw_ref[...]
    for s in range(n_new):
        acc = jnp.zeros((nb, D_CONV), F32)
        for kk in range(CONV_WIDTH):
            acc = acc + cw[kk:kk + 1] * ext_ref[s + kk]
        y_ref[s * nb:(s + 1) * nb] = acc
    conv_n = _conv_tail(y_ref[...] + cb_ref[...], lng_ref[...], lnb_ref[...], gconv_ref[...])
    upd = jnp.dot(attn_n, wout_ref[0:D_ATTN], preferred_element_type=F32)
    upd = upd + jnp.dot(conv_n, wout_ref[D_ATTN:], preferred_element_type=F32)
    o_ref[...] = x_ref[...] + upd


def _sample_post(x, lat, ext, w, *, nb):
    n = x.shape[0]
    return pl.pallas_call(
        functools.partial(_spost_kernel, nb=nb), name="sample_post",
        out_shape=jax.ShapeDtypeStruct((n, D_MODEL), F32),
        scratch_shapes=[pltpu.VMEM((n, D_CONV), F32)],
        compiler_params=pltpu.CompilerParams(vmem_limit_bytes=VMEM_LIMIT),
    )(x, lat, ext, w["wuv_pad"], w["conv_w"], w["conv_b"], w["ln_g"], w["ln_b"], w["gattn"],
      w["gconv"], w["wout"])


def _head_layout(g):
    return jnp.concatenate([g[QK_NOPE:], g[:QK_NOPE], jnp.zeros((HEAD_LANES - QK_DIM,), F32)])[None]


def _prep_weights(norm_mix_g, w_in, q_a_norm_g, w_uq, kv_a_norm_g, w_ukv, q_norm_g, k_norm_g,
                  conv_dw_w, conv_dw_b, conv_ln_g, conv_ln_b, out_norm_attn_g, out_norm_conv_g,
                  w_out, norm_ffn_g, w_ffn_up, ffn_dw_w, ffn_dw_b, w_ffn_down):
    half = QK_ROPE // 2
    o = 2 * D_CONV + Q_LORA + KV_LORA
    kp1, kp2 = w_in[:, o:o + half], w_in[:, o + half:o + QK_ROPE]
    win = jnp.concatenate([w_in, -kp2, kp1, jnp.zeros((D_MODEL, LANES - 2 * QK_ROPE), F32)], axis=1)

    uq = w_uq.reshape(Q_LORA, N_HEADS, QK_DIM)
    x1, x2 = uq[..., QK_NOPE:QK_NOPE + half], uq[..., QK_NOPE + half:]
    zq = jnp.zeros((Q_LORA, N_HEADS, HEAD_LANES - QK_DIM), F32)
    q_main = jnp.concatenate([x1, x2, uq[..., :QK_NOPE], zq], axis=-1)
    q_rot = jnp.concatenate([-x2, x1, jnp.zeros((Q_LORA, N_HEADS, HEAD_LANES - QK_ROPE), F32)], axis=-1)
    wuq = jnp.concatenate([q_main.reshape(Q_LORA, D_QK_PAD), q_rot.reshape(Q_LORA, D_QK_PAD)], axis=1)

    ukv = w_ukv.reshape(KV_LORA, N_HEADS, QK_NOPE + V_DIM)
    uk, uv = ukv[..., :QK_NOPE], ukv[..., QK_NOPE:]
    wuk = jnp.concatenate([jnp.zeros((KV_LORA, N_HEADS, QK_ROPE), F32), uk,
                           jnp.zeros((KV_LORA, N_HEADS, HEAD_LANES - QK_DIM), F32)],
                          axis=-1).reshape(KV_LORA, D_QK_PAD)
    wuvT = uv.reshape(KV_LORA, D_ATTN).T
    eye = jnp.eye(N_HEADS, dtype=F32)
    wuv_pad = jnp.einsum("chd,hg->hcgd", uv, eye).reshape(N_HEADS, KV_LORA, D_ATTN)
    aconst = uk.transpose(1, 2, 0).reshape(N_HEADS * QK_NOPE, KV_LORA)

    interleave = _interleave
    scale = QK_DIM ** -0.5 * LOG2E
    return dict(
        gmix=norm_mix_g[None], win=win.astype(BF16), gqa=q_a_norm_g[None], wuq=wuq.astype(BF16),
        gkva=kv_a_norm_g[None], wuk=wuk.astype(BF16), wuvT=wuvT.astype(BF16),
        gq=_head_layout(q_norm_g) * scale, gk=_head_layout(k_norm_g),
        wuv_pad=wuv_pad.astype(BF16), aconst=aconst.astype(BF16),
        conv_w=jnp.pad(conv_dw_w, ((0, CONV_HIST_PAD - CONV_WIDTH), (0, 0))), conv_b=conv_dw_b[None],
        conv_w8=jnp.broadcast_to(conv_dw_w[:, None, :], (CONV_WIDTH, SUBLANES, D_CONV)),
        ln_g=conv_ln_g[None], ln_b=conv_ln_b[None], gattn=out_norm_attn_g[None],
        gconv=out_norm_conv_g[None], wout=w_out.astype(BF16), gffn=norm_ffn_g[None],
        wup=interleave(w_ffn_up.astype(BF16)),
        ffn_w=jnp.pad(interleave(ffn_dw_w), ((0, SUBLANES - FFN_CONV_WIDTH), (0, 0))),
        ffn_b=interleave(ffn_dw_b)[None], wdn=w_ffn_down.astype(BF16),
    )


def _interleave(a):
    parts = []
    for c in range(D_FF // FFN_CHUNK):
        parts.append(a[..., c * FFN_CHUNK:(c + 1) * FFN_CHUNK])
        parts.append(a[..., D_FF + c * FFN_CHUNK:D_FF + (c + 1) * FFN_CHUNK])
    return jnp.concatenate(parts, axis=-1)


def _deinterleave(a):
    w2 = 2 * FFN_CHUNK
    nchunk = D_FF // FFN_CHUNK
    gate = [a[..., c * w2:c * w2 + FFN_CHUNK] for c in range(nchunk)]
    up = [a[..., c * w2 + FFN_CHUNK:(c + 1) * w2] for c in range(nchunk)]
    return jnp.concatenate(gate + up, axis=-1)


def _rope_lanes(pos):
    inv_freq = ROPE_THETA ** (-jnp.arange(0, QK_ROPE, 2, dtype=F32) / QK_ROPE)
    ang = pos.astype(F32)[:, None] * inv_freq[None, :]
    pad = jnp.zeros((pos.shape[0], LANES - QK_ROPE), F32)
    cos = jnp.concatenate([jnp.cos(ang), jnp.cos(ang), pad], axis=1)
    sin = jnp.concatenate([jnp.sin(ang), jnp.sin(ang), pad], axis=1)
    return cos, sin


def _prompt_layer(x, pos, conv_hist, ffn_hist, k_prefix, vT_prefix, w, *, t, tq, tr, tf, n_valid=None):
    cos, sin = _rope_lanes(pos)
    glu, q, k, vT, ckv, kpe = _proj(x, cos, sin, w, t=t, tq=tq)
    ao = _attention(q, k, vT, k_prefix, vT_prefix, tq=tq)
    x1 = _convout(x, glu, conv_hist, ao, w, t=tr)
    y, ffn_state = _ffn(x1, ffn_hist, w, t=tf, ts=1, has_prev=True, n_valid=n_valid)
    return y, glu, k, vT, ckv, kpe, ffn_state


def kernel(x_prompt, x_sample, cache_ckv, cache_kpe, state_conv, state_ffn, page_table, meta_tokens,
           norm_mix_g, w_in, q_a_norm_g, w_uq, kv_a_norm_g, w_ukv, q_norm_g, k_norm_g, conv_dw_w,
           conv_dw_b, conv_ln_g, conv_ln_b, out_norm_attn_g, out_norm_conv_g, w_out, norm_ffn_g,
           w_ffn_up, ffn_dw_w, ffn_dw_b, w_ffn_down):
    assert norm_mix_g.shape[0] == 1, "single-layer step"
    w = _prep_weights(norm_mix_g[0], w_in[0], q_a_norm_g[0], w_uq[0], kv_a_norm_g[0], w_ukv[0],
                      q_norm_g[0], k_norm_g[0], conv_dw_w[0], conv_dw_b[0], conv_ln_g[0],
                      conv_ln_b[0], out_norm_attn_g[0], out_norm_conv_g[0], w_out[0], norm_ffn_g[0],
                      w_ffn_up[0], ffn_dw_w[0], ffn_dw_b[0], w_ffn_down[0])
    bp, seq, _ = x_prompt.shape
    tm = 256
    tp = 512
    tr = 256
    assert seq % tp == 0

    xm = jnp.pad(meta_tokens.astype(F32), ((0, tm - N_META), (0, 0)))[None]
    zc = jnp.zeros((1, CONV_HIST_PAD, D_CONV), F32)
    zf = jnp.zeros((1, FFN_HIST_PAD, 2 * D_FF), F32)
    _, glu_m, k_m, vT_m, ckv_m, kpe_m, st_m = _prompt_layer(
        xm, jnp.arange(tm), zc, zf, None, None, w, t=tm, tq=tm, tr=tm, tf=tm, n_valid=N_META)

    conv_hist = jnp.pad(glu_m[:, :N_META], ((0, 0), (CONV_HIST_PAD - N_META, 0), (0, 0)))
    ffn_hist = jnp.pad(st_m, ((0, 0), (FFN_HIST_PAD - (FFN_CONV_WIDTH - 1), 0), (0, 0)))
    y_prompt, glu_p, _, _, ckv_p, kpe_p, st_p = _prompt_layer(
        x_prompt, N_META + jnp.arange(seq), conv_hist, ffn_hist,
        k_m[:, :N_META], vT_m[:, 0, :, :N_META], w, t=tp, tq=tp, tr=tr, tf=tp)
    ckv_prompt = jnp.concatenate(
        [jnp.broadcast_to(ckv_m[:, :N_META], (bp, N_META, KV_LORA)), ckv_p], axis=1)[None]
    kpe_prompt = jnp.concatenate(
        [jnp.broadcast_to(kpe_m[:, :N_META], (bp, N_META, QK_ROPE)), kpe_p], axis=1)[None]
    conv_prompt = glu_p[:, seq - (CONV_WIDTH - 1):][None]
    ffn_prompt = _deinterleave(st_p)[None]

    nb, ns, _ = x_sample.shape
    n = nb * ns
    n_hist = FFN_CONV_WIDTH - 1
    assert ns >= n_hist and ns <= SUBLANES
    page = cache_ckv.shape[2]
    past_len = page_table.shape[1] * page
    xs = x_sample.transpose(1, 0, 2).reshape(1, n, D_MODEL)
    cos_s, sin_s = _rope_lanes(past_len + jnp.arange(n) // nb)
    glu_s, q_s, _, _, ckv_s, kpe_s = _proj(xs, cos_s, sin_s, w, t=n, tq=n)

    def to_seq_major(a):
        return a.reshape(ns, nb, a.shape[-1]).transpose(1, 0, 2)

    qlat = _qlat(q_s[0], w)
    qlat = qlat.reshape(N_HEADS, ns, nb, 2 * LANES).transpose(2, 1, 0, 3).reshape(nb, ns * N_HEADS, 2 * LANES)
    ckv_sm, kpe_sm = to_seq_major(ckv_s[0]), to_seq_major(kpe_s[0])
    newc = jnp.pad(ckv_sm, ((0, 0), (0, page - ns), (0, 0)))
    newp_t = jnp.pad(kpe_sm.transpose(0, 2, 1), ((0, 0), (0, 0), (0, page - ns)))
    lat = _sample_attention(page_table, cache_ckv, jnp.swapaxes(cache_kpe, 2, 3), newc, newp_t,
                            w["aconst"], qlat, group=SAMPLE_GROUP, nslot=SAMPLE_SLOTS)
    lat = lat.reshape(nb, ns, N_HEADS, KV_LORA).transpose(2, 1, 0, 3).reshape(N_HEADS, n, KV_LORA)
    ext = jnp.concatenate([state_conv[0].transpose(1, 0, 2), glu_s[0].reshape(ns, nb, D_CONV)], axis=0)
    x1_s = _sample_post(xs[0], lat, ext, w, nb=nb)
    hist_s = _interleave(state_ffn[0]).transpose(1, 0, 2).reshape(1, n_hist * nb, 2 * D_FF)
    y_s, st_s = _ffn(x1_s[None], hist_s, w, t=n, ts=nb, has_prev=False)

    y_sample = to_seq_major(y_s[0])
    conv_sample = jnp.concatenate([state_conv[0][:, ns:], to_seq_major(glu_s[0])], axis=1)[None]
    ffn_sample = _deinterleave(st_s[0]).reshape(n_hist, nb, 2 * D_FF).transpose(1, 0, 2)[None]
    return (y_prompt, y_sample, ckv_prompt, kpe_prompt, conv_prompt, ffn_prompt,
            ckv_sm[None], kpe_sm[None], conv_sample, ffn_sample)
```

```python
import functools

import jax
import jax.numpy as jnp
from jax import lax
from jax.experimental import pallas as pl
from jax.experimental.pallas import tpu as pltpu

F32 = jnp.float32
BF16 = jnp.bfloat16

D_MODEL = 1024
N_META = 16
D_CONV = 512
CONV_WIDTH = 31
N_HEADS = 8
QK_NOPE = 64
QK_ROPE = 32
QK_DIM = QK_NOPE + QK_ROPE
V_DIM = 64
D_ATTN = N_HEADS * V_DIM
Q_LORA = 256
KV_LORA = 128
ROPE_THETA = 10000.0
D_FF = 2816
FFN_CONV_WIDTH = 3
EPS = 1e-6

LANES = 128
SUBLANES = 8
HEAD_LANES = LANES
D_QK_PAD = N_HEADS * HEAD_LANES
D_IN_PAD = 2 * D_CONV + Q_LORA + KV_LORA + LANES
CONV_HIST_PAD = 32
FFN_HIST_PAD = 16
FFN_CHUNK = 256
V_ROWS = 80
LOG2E = 1.4426950408889634
ATTN_HEADS_PER_STEP = 8
SAMPLE_GROUP = 32
SAMPLE_SLOTS = 4
VMEM_LIMIT = 56 * 1024 * 1024


def _cparams(sem):
    return pltpu.CompilerParams(dimension_semantics=sem, vmem_limit_bytes=VMEM_LIMIT)


def _rms_scale(x, n):
    return lax.rsqrt(jnp.sum(x * x, axis=-1, keepdims=True) * (1.0 / n) + EPS)


def _const_spec(shape):
    nd = len(shape)
    return pl.BlockSpec(shape, lambda *_: (0,) * nd)


def _proj_kernel(x_ref, cos_ref, sin_ref, gmix_ref, win_ref, gqa_ref, wuq_ref, gkva_ref, wuk_ref,
                 wuvT_ref, gq_ref, gk_ref, glu_ref, q_ref, k_ref, vT_ref, ckv_ref, kpe_ref, *, tq, th):
    t = x_ref.shape[1]
    lane = lax.broadcasted_iota(jnp.int32, (1, LANES), 1)
    nope = jnp.where((lane >= QK_ROPE) & (lane < QK_DIM), 1.0, 0.0).astype(F32)
    gq = gq_ref[...]
    gk = gk_ref[...]
    pad_row = lax.broadcasted_iota(jnp.int32, (V_ROWS - V_DIM, th), 0)
    ones_rows = jnp.where(pad_row == 0, 1.0, 0.0).astype(BF16)

    def in_proj(r0):
        x = x_ref[0, r0:r0 + th]
        h = (x * _rms_scale(x, D_MODEL) * gmix_ref[...]).astype(BF16)
        return jnp.dot(h, win_ref[...], preferred_element_type=F32)

    def low_rank(r0, proj):
        rows = slice(r0, r0 + th)
        glu_ref[0, rows] = proj[:, :D_CONV] * jax.nn.sigmoid(proj[:, D_CONV:2 * D_CONV])
        o = 2 * D_CONV
        cq = proj[:, o:o + Q_LORA]
        ckv_raw = proj[:, o + Q_LORA:o + Q_LORA + KV_LORA]
        kp = proj[:, o + Q_LORA + KV_LORA:]
        cqn = (cq * _rms_scale(cq, Q_LORA) * gqa_ref[...]).astype(BF16)
        qq = jnp.dot(cqn, wuq_ref[...], preferred_element_type=F32)
        ckv = ckv_raw * _rms_scale(ckv_raw, KV_LORA) * gkva_ref[...]
        ckv_ref[0, rows] = ckv
        cos = cos_ref[rows]
        sin = sin_ref[rows]
        kpe = kp * cos + pltpu.roll(kp, LANES - QK_ROPE, 1) * sin
        kpe_ref[0, rows] = kpe[:, :QK_ROPE]
        ckv_b = ckv.astype(BF16)
        kn = jnp.dot(ckv_b, wuk_ref[...], preferred_element_type=F32)
        vT = lax.dot_general(wuvT_ref[...], ckv_b, (((1,), (1,)), ((), ())),
                             preferred_element_type=F32).astype(BF16)
        j, c0 = divmod(r0, tq)
        for hd in range(N_HEADS):
            vT_ref[0, j, hd * V_ROWS:hd * V_ROWS + V_DIM, c0:c0 + th] = vT[hd * V_DIM:(hd + 1) * V_DIM]
            vT_ref[0, j, hd * V_ROWS + V_DIM:(hd + 1) * V_ROWS, c0:c0 + th] = ones_rows
        return qq, kn, kpe, cos, sin

    def heads_out(r0, qq, kn, kpe, cos, sin):
        rows = slice(r0, r0 + th)
        cosq = cos + nope
        for hd in range(N_HEADS):
            sl = slice(hd * HEAD_LANES, (hd + 1) * HEAD_LANES)
            qh = qq[:, sl] * cosq + qq[:, D_QK_PAD + hd * HEAD_LANES:D_QK_PAD + (hd + 1) * HEAD_LANES] * sin
            q_ref[0, rows, sl] = (qh * _rms_scale(qh, QK_DIM) * gq).astype(BF16)
            kh = kn[:, sl] + kpe
            k_ref[0, rows, sl] = (kh * _rms_scale(kh, QK_DIM) * gk).astype(BF16)

    starts = range(0, t, th)
    projs = [in_proj(r0) for r0 in starts]
    mids = [low_rank(r0, p) for r0, p in zip(starts, projs)]
    for r0, mid in zip(starts, mids):
        heads_out(r0, *mid)


def _proj(x, cos, sin, w, *, t, tq):
    b, l, _ = x.shape
    th = t // 2 if (t // 2) % LANES == 0 and tq % (t // 2) == 0 else t
    kern = functools.partial(_proj_kernel, tq=tq, th=th)
    out_shape = (
        jax.ShapeDtypeStruct((b, l, D_CONV), F32),
        jax.ShapeDtypeStruct((b, l, D_QK_PAD), BF16),
        jax.ShapeDtypeStruct((b, l, D_QK_PAD), BF16),
        jax.ShapeDtypeStruct((b, l // tq, N_HEADS * V_ROWS, tq), BF16),
        jax.ShapeDtypeStruct((b, l, KV_LORA), F32),
        jax.ShapeDtypeStruct((b, l, QK_ROPE), F32),
    )
    row = lambda bi, i: (bi, i, 0)
    in_specs = [
        pl.BlockSpec((1, t, D_MODEL), row),
        pl.BlockSpec((t, LANES), lambda bi, i: (i, 0)),
        pl.BlockSpec((t, LANES), lambda bi, i: (i, 0)),
        _const_spec((1, D_MODEL)),
        _const_spec((D_MODEL, D_IN_PAD)),
        _const_spec((1, Q_LORA)),
        _const_spec((Q_LORA, 2 * D_QK_PAD)),
        _const_spec((1, KV_LORA)),
        _const_spec((KV_LORA, D_QK_PAD)),
        _const_spec((D_ATTN, KV_LORA)),
        _const_spec((1, LANES)),
        _const_spec((1, LANES)),
    ]
    out_specs = (
        pl.BlockSpec((1, t, D_CONV), row),
        pl.BlockSpec((1, t, D_QK_PAD), row),
        pl.BlockSpec((1, t, D_QK_PAD), row),
        pl.BlockSpec((1, t // tq, N_HEADS * V_ROWS, tq), lambda bi, i: (bi, i, 0, 0)),
        pl.BlockSpec((1, t, KV_LORA), row),
        pl.BlockSpec((1, t, QK_ROPE), row),
    )
    return pl.pallas_call(
        kern, name=f"proj_b{b}", grid=(b, l // t), in_specs=in_specs, out_specs=out_specs,
        out_shape=out_shape,
        compiler_params=_cparams(("parallel", "parallel")),
    )(x, cos, sin, w["gmix"], w["win"], w["gqa"], w["wuq"], w["gkva"], w["wuk"], w["wuvT"],
      w["gq"], w["gk"])


def _softmax_update(m, acc, s, vT):
    m_new = jnp.maximum(m, jnp.max(s, axis=0, keepdims=True))
    p = jnp.exp2(s - m_new).astype(BF16)
    acc = acc * jnp.exp2(m - m_new) + jnp.dot(vT, p, preferred_element_type=F32)
    return m_new, acc


def _attn_kernel(*refs, tq, nq, n_prefix, nh):
    if n_prefix:
        q_ref, k_ref, vT_ref, kp_ref, vTp_ref, o_ref, oT_ref, s0_ref = refs
    else:
        q_ref, k_ref, vT_ref, o_ref, oT_ref, s0_ref = refs
    tb = (((1,), (1,)), ((), ()))
    row_i = lax.broadcasted_iota(jnp.int32, (tq, tq), 0)
    col_i = lax.broadcasted_iota(jnp.int32, (tq, tq), 1)
    causal = row_i <= col_i
    heads = range(nh)
    lanes = [slice(h * HEAD_LANES, (h + 1) * HEAD_LANES) for h in heads]
    vrows = [slice(h * V_ROWS, (h + 1) * V_ROWS) for h in heads]

    def q_body(qi, carry):
        q0 = pl.multiple_of(qi * tq, tq)
        q_t = [q_ref[0, pl.ds(q0, tq), lanes[h]] for h in heads]

        def scores(k0, h):
            return lax.dot_general(k_ref[0, pl.ds(k0, tq), lanes[h]], q_t[h], tb,
                                   preferred_element_type=F32)

        s0_ref[...] = scores(0, 0)
        state = []
        for h in heads:
            if n_prefix:
                s0 = lax.dot_general(kp_ref[0, :, lanes[h]], q_t[h], tb, preferred_element_type=F32)
                m = jnp.max(s0, axis=0, keepdims=True)
                acc = jnp.dot(vTp_ref[0, vrows[h], :], jnp.exp2(s0 - m).astype(BF16),
                              preferred_element_type=F32)
            else:
                m = jnp.full((1, tq), -jnp.inf, F32)
                acc = jnp.zeros((V_ROWS, tq), F32)
            state += [m, acc]

        def k_body(kj, c):
            k0 = pl.multiple_of(kj * tq, tq)
            out = []
            s_next = s0_ref[...]
            for h in heads:
                s = s_next
                if h + 1 < nh:
                    s_next = scores(k0, h + 1)
                else:
                    s0_ref[...] = scores(k0 + tq, 0)
                out += _softmax_update(c[2 * h], c[2 * h + 1], s, vT_ref[0, kj, vrows[h], :])
            return tuple(out)

        state = lax.fori_loop(0, qi, k_body, tuple(state))
        hq = tq // 2

        def diag_scores(h):
            sa = lax.dot_general(k_ref[0, pl.ds(q0, hq), lanes[h]], q_t[h], tb,
                                 preferred_element_type=F32)
            sb = lax.dot_general(k_ref[0, pl.ds(q0 + hq, hq), lanes[h]], q_t[h][hq:], tb,
                                 preferred_element_type=F32)
            return sa, sb

        s0 = s0_ref[...]
        d_next = (s0[:hq], s0[hq:, hq:])
        for h in heads:
            sa, sb = d_next
            if h + 1 < nh:
                d_next = diag_scores(h + 1)
            sa = jnp.where(causal[:hq], sa, -jnp.inf)
            sb = jnp.where(causal[hq:, hq:], sb, -jnp.inf)
            m, acc = _softmax_update(state[2 * h], state[2 * h + 1], sa, vT_ref[0, qi, vrows[h], 0:hq])
            _, acc_hi = _softmax_update(m[:, hq:], acc[:, hq:], sb, vT_ref[0, qi, vrows[h], hq:tq])
            oT_ref[qi, h * V_DIM:(h + 1) * V_DIM, 0:hq] = acc[:V_DIM, :hq] / acc[V_DIM:V_DIM + 1, :hq]
            oT_ref[qi, h * V_DIM:(h + 1) * V_DIM, hq:tq] = acc_hi[:V_DIM] / acc_hi[V_DIM:V_DIM + 1]
        return carry

    lax.fori_loop(0, nq, q_body, 0)
    for qi in range(nq):
        o_ref[0, qi * tq:(qi + 1) * tq, :] = oT_ref[qi].T


def _attention(q, k, vT, k_prefix=None, vT_prefix=None, *, tq):
    b, l, _ = q.shape
    nq = l // tq
    n_prefix = 0 if k_prefix is None else k_prefix.shape[1]
    nh = ATTN_HEADS_PER_STEP
    kern = functools.partial(_attn_kernel, tq=tq, nq=nq, n_prefix=n_prefix, nh=nh)
    in_specs = [
        pl.BlockSpec((1, l, nh * HEAD_LANES), lambda bi, g: (bi, 0, g)),
        pl.BlockSpec((1, l, nh * HEAD_LANES), lambda bi, g: (bi, 0, g)),
        pl.BlockSpec((1, nq, nh * V_ROWS, tq), lambda bi, g: (bi, 0, g, 0)),
    ]
    args = [q, k, vT]
    if n_prefix:
        in_specs += [
            pl.BlockSpec((1, n_prefix, nh * HEAD_LANES), lambda bi, g: (0, 0, g)),
            pl.BlockSpec((1, nh * V_ROWS, n_prefix), lambda bi, g: (0, g, 0)),
        ]
        args += [k_prefix, vT_prefix]
    return pl.pallas_call(
        kern, name=f"attn_b{b}", grid=(b, N_HEADS // nh), in_specs=in_specs,
        out_specs=pl.BlockSpec((1, l, nh * V_DIM), lambda bi, g: (bi, 0, g)),
        out_shape=jax.ShapeDtypeStruct((b, l, D_ATTN), F32),
        scratch_shapes=[pltpu.VMEM((nq, nh * V_DIM, tq), F32), pltpu.VMEM((tq, tq), F32)],
        compiler_params=_cparams(("parallel", "parallel")),
    )(*args)


def _conv_tail(y, lng, lnb, gconv):
    mu = jnp.mean(y, axis=-1, keepdims=True)
    yc = y - mu
    var = jnp.mean(yc * yc, axis=-1, keepdims=True)
    z = yc * lax.rsqrt(var + EPS) * lng + lnb
    z = z * jax.nn.sigmoid(z)
    return (z * _rms_scale(z, D_CONV) * gconv).astype(BF16)


def _convout_kernel(x_ref, glu_ref, prev_ref, hist_ref, ao_ref, cw_ref, cb_ref, lng_ref, lnb_ref,
                    gattn_ref, gconv_ref, wout_ref, o_ref, ext_ref, y_ref, *, rc):
    i = pl.program_id(1)
    t = glu_ref.shape[1]
    ext_ref[0:CONV_HIST_PAD] = jnp.where(i == 0, hist_ref[0], prev_ref[0])
    ext_ref[CONV_HIST_PAD:] = glu_ref[0]
    base = CONV_HIST_PAD - (CONV_WIDTH - 1)
    nt = rc // SUBLANES
    for cc in range(D_CONV // LANES):
        cs = slice(cc * LANES, (cc + 1) * LANES)
        for r0 in range(0, t, rc):
            acc = jnp.zeros((nt, SUBLANES, LANES), F32)
            for res in range(SUBLANES):
                taps = [kk for kk in range(CONV_WIDTH) if (base + kk) % SUBLANES == res]
                shifted = ext_ref[r0 + res:r0 + base + taps[-1] + rc, cs].reshape(-1, SUBLANES, LANES)
                for kk in taps:
                    wk = cw_ref[kk, :, cs][None]
                    off = (base + kk - res) // SUBLANES
                    acc = acc + wk * shifted[off:off + nt]
            y_ref[r0:r0 + rc, cs] = acc.reshape(rc, LANES)
    conv_n = _conv_tail(y_ref[...] + cb_ref[...], lng_ref[...], lnb_ref[...], gconv_ref[...])
    ao = ao_ref[0]
    attn_n = (ao * _rms_scale(ao, D_ATTN) * gattn_ref[...]).astype(BF16)
    upd = jnp.dot(attn_n, wout_ref[0:D_ATTN], preferred_element_type=F32)
    upd = upd + jnp.dot(conv_n, wout_ref[D_ATTN:], preferred_element_type=F32)
    o_ref[0] = x_ref[0] + upd


def _convout(x, glu, hist, ao, w, *, t):
    b, l, _ = x.shape
    per_batch_hist = hist.shape[0] > 1
    hb = t // CONV_HIST_PAD
    row = lambda bi, i: (bi, i, 0)
    in_specs = [
        pl.BlockSpec((1, t, D_MODEL), row),
        pl.BlockSpec((1, t, D_CONV), row),
        pl.BlockSpec((1, CONV_HIST_PAD, D_CONV), lambda bi, i: (bi, jnp.maximum(i * hb - 1, 0), 0)),
        pl.BlockSpec((1, CONV_HIST_PAD, D_CONV),
                     (lambda bi, i: (bi, 0, 0)) if per_batch_hist else (lambda bi, i: (0, 0, 0))),
        pl.BlockSpec((1, t, D_ATTN), row),
        _const_spec((CONV_WIDTH, SUBLANES, D_CONV)),
        _const_spec((1, D_CONV)),
        _const_spec((1, D_CONV)),
        _const_spec((1, D_CONV)),
        _const_spec((1, D_ATTN)),
        _const_spec((1, D_CONV)),
        _const_spec((D_MODEL, D_MODEL)),
    ]
    return pl.pallas_call(
        functools.partial(_convout_kernel, rc=min(t, 128)), name=f"convout_b{b}",
        grid=(b, l // t), in_specs=in_specs,
        out_specs=pl.BlockSpec((1, t, D_MODEL), row),
        out_shape=jax.ShapeDtypeStruct((b, l, D_MODEL), F32),
        scratch_shapes=[pltpu.VMEM((t + CONV_HIST_PAD, D_CONV), F32), pltpu.VMEM((t, D_CONV), F32)],
        compiler_params=_cparams(("parallel", "arbitrary")),
    )(x, glu, glu, hist, ao, w["conv_w8"], w["conv_b"], w["ln_g"], w["ln_b"], w["gattn"],
      w["gconv"], w["wout"])


def _ffn_kernel(*refs, ts, has_prev, st0):
    if has_prev:
        x_ref, prev_ref, hist_ref, g_ref, wup_ref, cw_ref, cb_ref, wdn_ref, o_ref, st_ref, act_ref = refs
    else:
        x_ref, hist_ref, g_ref, wup_ref, cw_ref, cb_ref, wdn_ref, o_ref, st_ref, act_ref = refs
    i = pl.program_id(1)
    x = x_ref[0]
    t = x.shape[0]
    hr = hist_ref.shape[1]
    g = g_ref[...]
    if has_prev:
        xe = jnp.concatenate([prev_ref[0], x], axis=0)
    else:
        xe = x
    hx = (xe * _rms_scale(xe, D_MODEL) * g).astype(BF16)
    cw = cw_ref[...]
    cb = cb_ref[...]
    w2 = 2 * FFN_CHUNK
    for c in range(D_FF // FFN_CHUNK):
        cs = slice(c * w2, (c + 1) * w2)
        u = jnp.dot(hx, wup_ref[:, cs], preferred_element_type=F32)
        hist = hist_ref[0, :, cs]
        if has_prev:
            u_ext = jnp.concatenate([jnp.where(i == 0, hist, u[:hr]), u[hr:]], axis=0)
        else:
            u_ext = jnp.concatenate([hist, u], axis=0)
        st_ref[0, :, cs] = u_ext[st0:st0 + 2 * ts]
        uc = (cw[0:1, cs] * u_ext[hr - 2 * ts:hr - 2 * ts + t]
              + cw[1:2, cs] * u_ext[hr - ts:hr - ts + t]
              + cw[2:3, cs] * u_ext[hr:] + cb[:, cs])
        gate = uc[:, :FFN_CHUNK]
        act = gate * jax.nn.sigmoid(gate) * uc[:, FFN_CHUNK:]
        act_ref[:, c * FFN_CHUNK:(c + 1) * FFN_CHUNK] = act.astype(BF16)
    o_ref[0] = x + jnp.dot(act_ref[...], wdn_ref[...], preferred_element_type=F32)


def _ffn(x, hist, w, *, t, ts, has_prev, n_valid=None):
    b, l, _ = x.shape
    hr = hist.shape[1]
    st0 = hr + (t if n_valid is None else n_valid) - 2 * ts
    per_batch_hist = hist.shape[0] > 1
    row = lambda bi, i: (bi, i, 0)
    in_specs = [pl.BlockSpec((1, t, D_MODEL), row)]
    args = [x]
    scratch = [pltpu.VMEM((t, D_FF), BF16)]
    if has_prev:
        hb = t // hr
        in_specs.append(pl.BlockSpec((1, hr, D_MODEL), lambda bi, i: (bi, jnp.maximum(i * hb - 1, 0), 0)))
        args.append(x)
    in_specs += [
        pl.BlockSpec((1, hr, 2 * D_FF),
                     (lambda bi, i: (bi, 0, 0)) if per_batch_hist else (lambda bi, i: (0, 0, 0))),
        _const_spec((1, D_MODEL)),
        _const_spec((D_MODEL, 2 * D_FF)),
        _const_spec((SUBLANES, 2 * D_FF)),
        _const_spec((1, 2 * D_FF)),
        _const_spec((D_FF, D_MODEL)),
    ]
    args += [hist, w["gffn"], w["wup"], w["ffn_w"], w["ffn_b"], w["wdn"]]
    out, st = pl.pallas_call(
        functools.partial(_ffn_kernel, ts=ts, has_prev=has_prev, st0=st0), name=f"ffn_b{b}_ts{ts}",
        grid=(b, l // t), in_specs=in_specs,
        out_specs=(pl.BlockSpec((1, t, D_MODEL), row),
                   pl.BlockSpec((1, 2 * ts, 2 * D_FF), lambda bi, i: (bi, 0, 0))),
        out_shape=(jax.ShapeDtypeStruct((b, l, D_MODEL), F32),
                   jax.ShapeDtypeStruct((b, 2 * ts, 2 * D_FF), F32)),
        scratch_shapes=scratch,
        compiler_params=_cparams(("parallel", "arbitrary")),
    )(*args)
    return out, st


def _qlat_kernel(q_ref, gk_ref, wuk_ref, o_ref):
    gk = gk_ref[...]
    n = q_ref.shape[0]
    for hd in range(N_HEADS):
        sl = slice(hd * HEAD_LANES, (hd + 1) * HEAD_LANES)
        qg = q_ref[:, sl].astype(F32) * gk
        lat = lax.dot_general(qg.astype(BF16), wuk_ref[:, sl], (((1,), (1,)), ((), ())),
                              preferred_element_type=F32)
        lane = lax.broadcasted_iota(jnp.int32, (1, LANES), 1)
        rope = jnp.where(lane < QK_ROPE, qg, 0.0)
        o_ref[hd] = jnp.concatenate([lat, rope], axis=1).astype(BF16)


def _qlat(q, w):
    n = q.shape[0]
    return pl.pallas_call(
        _qlat_kernel, name="sample_qlat",
        out_shape=jax.ShapeDtypeStruct((N_HEADS, n, 2 * LANES), BF16),
    )(q, w["gk"], w["wuk"])


def _sattn_kernel(pt_ref, ckv_hbm, kpe_hbm, newc_ref, newp_ref, aconst_ref, qlat_ref, o_ref,
                  cbuf, pbuf, sem, a_ref, *, group, page, n_new, n_groups, nslot):
    b = pl.program_id(0)
    nb = pl.num_programs(0)
    per_round = nslot // 2
    n_rounds = n_groups // per_round
    nconst = aconst_ref.shape[0]
    rows = a_ref.shape[0] - nconst
    tb = (((1,), (1,)), ((), ()))

    def group_copies(bb, g, slot):
        cps = []
        for j in range(group):
            pg = pt_ref[bb, g * group + j]
            cps.append(pltpu.make_async_copy(
                ckv_hbm.at[0, pg], cbuf.at[slot, pl.ds(j * page, page), :], sem.at[0, slot]))
            cps.append(pltpu.make_async_copy(
                kpe_hbm.at[0, pg], pbuf.at[slot, :, pl.ds(j * page, page)], sem.at[1, slot]))
        return cps

    def round_copies(bb, rd, half):
        return [cp for u in range(per_round)
                for cp in group_copies(bb, rd * per_round + u, half * per_round + u)]

    @pl.when(b == 0)
    def _():
        a_ref[0:nconst] = aconst_ref[...]
        for cp in round_copies(0, 0, 0):
            cp.start()

    a_ref[nconst:] = qlat_ref[0, :, 0:KV_LORA]
    q_rope = qlat_ref[0, :, KV_LORA:KV_LORA + QK_ROPE]

    def expand(ckv_f32):
        half = nconst // 2
        ckv_b = ckv_f32.astype(BF16)
        r0 = lax.dot_general(a_ref[0:half], ckv_b, tb, preferred_element_type=F32)
        r1 = lax.dot_general(a_ref[half:], ckv_b, tb, preferred_element_type=F32)
        return ckv_b, r0, r1

    def chain(expanded, kpe_t, mask=None):
        ckv_b, r0, r1 = expanded
        tk = ckv_b.shape[0]
        half = nconst // 2
        kn = jnp.concatenate([r0, r1[0:half]], axis=0)
        ss = jnp.sum((kn * kn).reshape(N_HEADS, QK_NOPE, tk), axis=1)
        ss = ss + jnp.sum(kpe_t * kpe_t, axis=0, keepdims=True)
        rinv = lax.rsqrt(ss * (1.0 / QK_DIM) + EPS)
        sc = r1[half:] + jnp.dot(q_rope, kpe_t.astype(BF16), preferred_element_type=F32)
        s = sc * jnp.concatenate([rinv] * n_new, axis=0)
        if mask is not None:
            s = jnp.where(mask, s, -jnp.inf)
        m = jnp.max(s, axis=1, keepdims=True)
        p = jnp.exp2(s - m)
        return m, jnp.sum(p, axis=1, keepdims=True), jnp.dot(p.astype(BF16), ckv_b,
                                                             preferred_element_type=F32)

    def merge(state, part):
        m_old, l_old, acc_old = state
        m, l, acc = part
        m_new = jnp.maximum(m_old, m)
        w_old = jnp.exp2(m_old - m_new)
        w = jnp.exp2(m - m_new)
        return m_new, l_old * w_old + l * w, acc_old * w_old + acc * w

    def two_rounds(it, state):
        for half in range(2):
            rd = it * 2 + half
            wrap = rd + 1 >= n_rounds
            bb = jnp.minimum(jnp.where(wrap, b + 1, b), nb - 1)
            for cp in round_copies(bb, jnp.where(wrap, 0, rd + 1), 1 - half):
                cp.start()
            for cp in round_copies(b, rd, half):
                cp.wait()
            base = half * per_round
            nxt = expand(cbuf[base])
            for u in range(per_round):
                cur = nxt
                if u + 1 < per_round:
                    nxt = expand(cbuf[base + u + 1])
                state = merge(state, chain(cur, pbuf[base + u]))
        return state

    state = (jnp.full((rows, 1), -jnp.inf, F32), jnp.zeros((rows, 1), F32),
             jnp.zeros((rows, KV_LORA), F32))
    state = lax.fori_loop(0, n_rounds // 2, two_rounds, state)

    tok = lax.broadcasted_iota(jnp.int32, (rows, page), 1)
    step = lax.broadcasted_iota(jnp.int32, (rows, page), 0) // N_HEADS
    _, l, acc = merge(state, chain(expand(newc_ref[0]), newp_ref[0], mask=tok <= step))
    o_ref[0] = acc / l

    @pl.when(b == nb - 1)
    def _():
        for cp in round_copies(b, 0, 0):
            cp.wait()


def _sample_attention(page_table, cache_ckv, cache_kpe_t, newc, newp_t, aconst, qlat, *, group, nslot):
    nb, n_pages = page_table.shape
    page = cache_ckv.shape[2]
    nrow = qlat.shape[1]
    n_new = nrow // N_HEADS
    nconst = aconst.shape[0]
    n_groups = n_pages // group
    assert n_pages % group == 0 and n_groups % nslot == 0
    in_specs = [
        pl.BlockSpec(memory_space=pl.ANY),
        pl.BlockSpec(memory_space=pl.ANY),
        pl.BlockSpec((1, page, KV_LORA), lambda b, pt: (b, 0, 0)),
        pl.BlockSpec((1, QK_ROPE, page), lambda b, pt: (b, 0, 0)),
        pl.BlockSpec((nconst, KV_LORA), lambda b, pt: (0, 0)),
        pl.BlockSpec((1, nrow, 2 * LANES), lambda b, pt: (b, 0, 0)),
    ]
    grid_spec = pltpu.PrefetchScalarGridSpec(
        num_scalar_prefetch=1, grid=(nb,), in_specs=in_specs,
        out_specs=pl.BlockSpec((1, nrow, KV_LORA), lambda b, pt: (b, 0, 0)),
        scratch_shapes=[
            pltpu.VMEM((nslot, group * page, KV_LORA), F32),
            pltpu.VMEM((nslot, QK_ROPE, group * page), F32),
            pltpu.SemaphoreType.DMA((2, nslot)),
            pltpu.VMEM((nconst + nrow, KV_LORA), BF16),
        ])
    return pl.pallas_call(
        functools.partial(_sattn_kernel, group=group, page=page, n_new=n_new, n_groups=n_groups,
                          nslot=nslot),
        name="sample_attn", grid_spec=grid_spec,
        out_shape=jax.ShapeDtypeStruct((nb, nrow, KV_LORA), F32),
        compiler_params=_cparams(("arbitrary",)),
    )(page_table, cache_ckv, cache_kpe_t, newc, newp_t, aconst, qlat)


def _spost_kernel(x_ref, lat_ref, ext_ref, wuvp_ref, cw_ref, cb_ref, lng_ref, lnb_ref, gattn_ref,
                  gconv_ref, wout_ref, o_ref, y_ref, *, nb):
    n = x_ref.shape[0]
    n_new = n // nb
    ao = jnp.zeros((n, D_ATTN), F32)
    for hd in range(N_HEADS):
        ao = ao + jnp.dot(lat_ref[hd].astype(BF16), wuvp_ref[hd], preferred_element_type=F32)
    attn_n = (ao * _rms_scale(ao, D_ATTN) * gattn_ref[...]).astype(BF16)
    cw = cw_ref[...]
    for s in range(n_new):
        acc = jnp.zeros((nb, D_CONV), F32)
        for kk in range(CONV_WIDTH):
            acc = acc + cw[kk:kk + 1] * ext_ref[s + kk]
        y_ref[s * nb:(s + 1) * nb] = acc
    conv_n = _conv_tail(y_ref[...] + cb_ref[...], lng_ref[...], lnb_ref[...], gconv_ref[...])
    upd = jnp.dot(attn_n, wout_ref[0:D_ATTN], preferred_element_type=F32)
    upd = upd + jnp.dot(conv_n, wout_ref[D_ATTN:], preferred_element_type=F32)
    o_ref[...] = x_ref[...] + upd


def _sample_post(x, lat, ext, w, *, nb):
    n = x.shape[0]
    return pl.pallas_call(
        functools.partial(_spost_kernel, nb=nb), name="sample_post",
        out_shape=jax.ShapeDtypeStruct((n, D_MODEL), F32),
        scratch_shapes=[pltpu.VMEM((n, D_CONV), F32)],
        compiler_params=pltpu.CompilerParams(vmem_limit_bytes=VMEM_LIMIT),
    )(x, lat, ext, w["wuv_pad"], w["conv_w"], w["conv_b"], w["ln_g"], w["ln_b"], w["gattn"],
      w["gconv"], w["wout"])


def _head_layout(g):
    return jnp.concatenate([g[QK_NOPE:], g[:QK_NOPE], jnp.zeros((HEAD_LANES - QK_DIM,), F32)])[None]


def _prep_weights(norm_mix_g, w_in, q_a_norm_g, w_uq, kv_a_norm_g, w_ukv, q_norm_g, k_norm_g,
                  conv_dw_w, conv_dw_b, conv_ln_g, conv_ln_b, out_norm_attn_g, out_norm_conv_g,
                  w_out, norm_ffn_g, w_ffn_up, ffn_dw_w, ffn_dw_b, w_ffn_down):
    half = QK_ROPE // 2
    o = 2 * D_CONV + Q_LORA + KV_LORA
    kp1, kp2 = w_in[:, o:o + half], w_in[:, o + half:o + QK_ROPE]
    win = jnp.concatenate([w_in, -kp2, kp1, jnp.zeros((D_MODEL, LANES - 2 * QK_ROPE), F32)], axis=1)

    uq = w_uq.reshape(Q_LORA, N_HEADS, QK_DIM)
    x1, x2 = uq[..., QK_NOPE:QK_NOPE + half], uq[..., QK_NOPE + half:]
    zq = jnp.zeros((Q_LORA, N_HEADS, HEAD_LANES - QK_DIM), F32)
    q_main = jnp.concatenate([x1, x2, uq[..., :QK_NOPE], zq], axis=-1)
    q_rot = jnp.concatenate([-x2, x1, jnp.zeros((Q_LORA, N_HEADS, HEAD_LANES - QK_ROPE), F32)], axis=-1)
    wuq = jnp.concatenate([q_main.reshape(Q_LORA, D_QK_PAD), q_rot.reshape(Q_LORA, D_QK_PAD)], axis=1)

    ukv = w_ukv.reshape(KV_LORA, N_HEADS, QK_NOPE + V_DIM)
    uk, uv = ukv[..., :QK_NOPE], ukv[..., QK_NOPE:]
    wuk = jnp.concatenate([jnp.zeros((KV_LORA, N_HEADS, QK_ROPE), F32), uk,
                           jnp.zeros((KV_LORA, N_HEADS, HEAD_LANES - QK_DIM), F32)],
                          axis=-1).reshape(KV_LORA, D_QK_PAD)
    wuvT = uv.reshape(KV_LORA, D_ATTN).T
    eye = jnp.eye(N_HEADS, dtype=F32)
    wuv_pad = jnp.einsum("chd,hg->hcgd", uv, eye).reshape(N_HEADS, KV_LORA, D_ATTN)
    aconst = uk.transpose(1, 2, 0).reshape(N_HEADS * QK_NOPE, KV_LORA)

    interleave = _interleave
    scale = QK_DIM ** -0.5 * LOG2E
    return dict(
        gmix=norm_mix_g[None], win=win.astype(BF16), gqa=q_a_norm_g[None], wuq=wuq.astype(BF16),
        gkva=kv_a_norm_g[None], wuk=wuk.astype(BF16), wuvT=wuvT.astype(BF16),
        gq=_head_layout(q_norm_g) * scale, gk=_head_layout(k_norm_g),
        wuv_pad=wuv_pad.astype(BF16), aconst=aconst.astype(BF16),
        conv_w=jnp.pad(conv_dw_w, ((0, CONV_HIST_PAD - CONV_WIDTH), (0, 0))), conv_b=conv_dw_b[None],
        conv_w8=jnp.broadcast_to(conv_dw_w[:, None, :], (CONV_WIDTH, SUBLANES, D_CONV)),
        ln_g=conv_ln_g[None], ln_b=conv_ln_b[None], gattn=out_norm_attn_g[None],
        gconv=out_norm_conv_g[None], wout=w_out.astype(BF16), gffn=norm_ffn_g[None],
        wup=interleave(w_ffn_up.astype(BF16)),
        ffn_w=jnp.pad(interleave(ffn_dw_w), ((0, SUBLANES - FFN_CONV_WIDTH), (0, 0))),
        ffn_b=interleave(ffn_dw_b)[None], wdn=w_ffn_down.astype(BF16),
    )


def _interleave(a):
    parts = []
    for c in range(D_FF // FFN_CHUNK):
        parts.append(a[..., c * FFN_CHUNK:(c + 1) * FFN_CHUNK])
        parts.append(a[..., D_FF + c * FFN_CHUNK:D_FF + (c + 1) * FFN_CHUNK])
    return jnp.concatenate(parts, axis=-1)


def _deinterleave(a):
    w2 = 2 * FFN_CHUNK
    nchunk = D_FF // FFN_CHUNK
    gate = [a[..., c * w2:c * w2 + FFN_CHUNK] for c in range(nchunk)]
    up = [a[..., c * w2 + FFN_CHUNK:(c + 1) * w2] for c in range(nchunk)]
    return jnp.concatenate(gate + up, axis=-1)


def _rope_lanes(pos):
    inv_freq = ROPE_THETA ** (-jnp.arange(0, QK_ROPE, 2, dtype=F32) / QK_ROPE)
    ang = pos.astype(F32)[:, None] * inv_freq[None, :]
    pad = jnp.zeros((pos.shape[0], LANES - QK_ROPE), F32)
    cos = jnp.concatenate([jnp.cos(ang), jnp.cos(ang), pad], axis=1)
    sin = jnp.concatenate([jnp.sin(ang), jnp.sin(ang), pad], axis=1)
    return cos, sin


def _prompt_layer(x, pos, conv_hist, ffn_hist, k_prefix, vT_prefix, w, *, t, tq, tr, tf, n_valid=None):
    cos, sin = _rope_lanes(pos)
    glu, q, k, vT, ckv, kpe = _proj(x, cos, sin, w, t=t, tq=tq)
    ao = _attention(q, k, vT, k_prefix, vT_prefix, tq=tq)
    x1 = _convout(x, glu, conv_hist, ao, w, t=tr)
    y, ffn_state = _ffn(x1, ffn_hist, w, t=tf, ts=1, has_prev=True, n_valid=n_valid)
    return y, glu, k, vT, ckv, kpe, ffn_state


def kernel(x_prompt, x_sample, cache_ckv, cache_kpe, state_conv, state_ffn, page_table, meta_tokens,
           norm_mix_g, w_in, q_a_norm_g, w_uq, kv_a_norm_g, w_ukv, q_norm_g, k_norm_g, conv_dw_w,
           conv_dw_b, conv_ln_g, conv_ln_b, out_norm_attn_g, out_norm_conv_g, w_out, norm_ffn_g,
           w_ffn_up, ffn_dw_w, ffn_dw_b, w_ffn_down):
    assert norm_mix_g.shape[0] == 1, "single-layer step"
    w = _prep_weights(norm_mix_g[0], w_in[0], q_a_norm_g[0], w_uq[0], kv_a_norm_g[0], w_ukv[0],
                      q_norm_g[0], k_norm_g[0], conv_dw_w[0], conv_dw_b[0], conv_ln_g[0],
                      conv_ln_b[0], out_norm_attn_g[0], out_norm_conv_g[0], w_out[0], norm_ffn_g[0],
                      w_ffn_up[0], ffn_dw_w[0], ffn_dw_b[0], w_ffn_down[0])
    bp, seq, _ = x_prompt.shape
    tm = 256
    tp = 512
    tr = 512
    assert seq % tp == 0

    xm = jnp.pad(meta_tokens.astype(F32), ((0, tm - N_META), (0, 0)))[None]
    zc = jnp.zeros((1, CONV_HIST_PAD, D_CONV), F32)
    zf = jnp.zeros((1, FFN_HIST_PAD, 2 * D_FF), F32)
    _, glu_m, k_m, vT_m, ckv_m, kpe_m, st_m = _prompt_layer(
        xm, jnp.arange(tm), zc, zf, None, None, w, t=tm, tq=tm, tr=tm, tf=tm, n_valid=N_META)

    conv_hist = jnp.pad(glu_m[:, :N_META], ((0, 0), (CONV_HIST_PAD - N_META, 0), (0, 0)))
    ffn_hist = jnp.pad(st_m, ((0, 0), (FFN_HIST_PAD - (FFN_CONV_WIDTH - 1), 0), (0, 0)))
    y_prompt, glu_p, _, _, ckv_p, kpe_p, st_p = _prompt_layer(
        x_prompt, N_META + jnp.arange(seq), conv_hist, ffn_hist,
        k_m[:, :N_META], vT_m[:, 0, :, :N_META], w, t=tp, tq=tp, tr=tr, tf=tp)
    ckv_prompt = jnp.concatenate(
        [jnp.broadcast_to(ckv_m[:, :N_META], (bp, N_META, KV_LORA)), ckv_p], axis=1)[None]
    kpe_prompt = jnp.concatenate(
        [jnp.broadcast_to(kpe_m[:, :N_META], (bp, N_META, QK_ROPE)), kpe_p], axis=1)[None]
    conv_prompt = glu_p[:, seq - (CONV_WIDTH - 1):][None]
    ffn_prompt = _deinterleave(st_p)[None]

    nb, ns, _ = x_sample.shape
    n = nb * ns
    n_hist = FFN_CONV_WIDTH - 1
    assert ns >= n_hist and ns <= SUBLANES
    page = cache_ckv.shape[2]
    past_len = page_table.shape[1] * page
    xs = x_sample.transpose(1, 0, 2).reshape(1, n, D_MODEL)
    cos_s, sin_s = _rope_lanes(past_len + jnp.arange(n) // nb)
    glu_s, q_s, _, _, ckv_s, kpe_s = _proj(xs, cos_s, sin_s, w, t=n, tq=n)

    def to_seq_major(a):
        return a.reshape(ns, nb, a.shape[-1]).transpose(1, 0, 2)

    qlat = _qlat(q_s[0], w)
    qlat = qlat.reshape(N_HEADS, ns, nb, 2 * LANES).transpose(2, 1, 0, 3).reshape(nb, ns * N_HEADS, 2 * LANES)
    ckv_sm, kpe_sm = to_seq_major(ckv_s[0]), to_seq_major(kpe_s[0])
    newc = jnp.pad(ckv_sm, ((0, 0), (0, page - ns), (0, 0)))
    newp_t = jnp.pad(kpe_sm.transpose(0, 2, 1), ((0, 0), (0, 0), (0, page - ns)))
    lat = _sample_attention(page_table, cache_ckv, jnp.swapaxes(cache_kpe, 2, 3), newc, newp_t,
                            w["aconst"], qlat, group=SAMPLE_GROUP, nslot=SAMPLE_SLOTS)
    lat = lat.reshape(nb, ns, N_HEADS, KV_LORA).transpose(2, 1, 0, 3).reshape(N_HEADS, n, KV_LORA)
    ext = jnp.concatenate([state_conv[0].transpose(1, 0, 2), glu_s[0].reshape(ns, nb, D_CONV)], axis=0)
    x1_s = _sample_post(xs[0], lat, ext, w, nb=nb)
    hist_s = _interleave(state_ffn[0]).transpose(1, 0, 2).reshape(1, n_hist * nb, 2 * D_FF)
    y_s, st_s = _ffn(x1_s[None], hist_s, w, t=n, ts=nb, has_prev=False)

    y_sample = to_seq_major(y_s[0])
    conv_sample = jnp.concatenate([state_conv[0][:, ns:], to_seq_major(glu_s[0])], axis=1)[None]
    ffn_sample = _deinterleave(st_s[0]).reshape(n_hist, nb, 2 * D_FF).transpose(1, 0, 2)[None]
    return (y_prompt, y_sample, ckv_prompt, kpe_prompt, conv_prompt, ffn_prompt,
            ckv_sm[None], kpe_sm[None], conv_sample, ffn_sample)
```

```python
import functools

import jax
import jax.numpy as jnp
from jax import lax
from jax.experimental import pallas as pl
from jax.experimental.pallas import tpu as pltpu

F32 = jnp.float32
BF16 = jnp.bfloat16

D_MODEL = 1024
N_META = 16
D_CONV = 512
CONV_WIDTH = 31
N_HEADS = 8
QK_NOPE = 64
QK_ROPE = 32
QK_DIM = QK_NOPE + QK_ROPE
V_DIM = 64
D_ATTN = N_HEADS * V_DIM
Q_LORA = 256
KV_LORA = 128
ROPE_THETA = 10000.0
D_FF = 2816
FFN_CONV_WIDTH = 3
EPS = 1e-6

LANES = 128
SUBLANES = 8
HEAD_LANES = LANES
D_QK_PAD = N_HEADS * HEAD_LANES
D_IN_PAD = 2 * D_CONV + Q_LORA + KV_LORA + LANES
CONV_HIST_PAD = 32
FFN_HIST_PAD = 16
FFN_CHUNK = 256
V_ROWS = 80
LOG2E = 1.4426950408889634
ATTN_HEADS_PER_STEP = 8
SAMPLE_GROUP = 32
SAMPLE_SLOTS = 4
VMEM_LIMIT = 56 * 1024 * 1024


def _cparams(sem):
    return pltpu.CompilerParams(dimension_semantics=sem, vmem_limit_bytes=VMEM_LIMIT)


def _rms_scale(x, n):
    return lax.rsqrt(jnp.sum(x * x, axis=-1, keepdims=True) * (1.0 / n) + EPS)


def _const_spec(shape):
    nd = len(shape)
    return pl.BlockSpec(shape, lambda *_: (0,) * nd)


def _proj_kernel(x_ref, cos_ref, sin_ref, gmix_ref, win_ref, gqa_ref, wuq_ref, gkva_ref, wuk_ref,
                 wuvT_ref, gq_ref, gk_ref, glu_ref, q_ref, k_ref, vT_ref, ckv_ref, kpe_ref, *, tq, th):
    t = x_ref.shape[1]
    lane = lax.broadcasted_iota(jnp.int32, (1, LANES), 1)
    nope = jnp.where((lane >= QK_ROPE) & (lane < QK_DIM), 1.0, 0.0).astype(F32)
    gq = gq_ref[...]
    gk = gk_ref[...]
    pad_row = lax.broadcasted_iota(jnp.int32, (V_ROWS - V_DIM, th), 0)
    ones_rows = jnp.where(pad_row == 0, 1.0, 0.0).astype(BF16)

    def in_proj(r0):
        x = x_ref[0, r0:r0 + th]
        h = (x * _rms_scale(x, D_MODEL) * gmix_ref[...]).astype(BF16)
        return jnp.dot(h, win_ref[...], preferred_element_type=F32)

    def low_rank(r0, proj):
        rows = slice(r0, r0 + th)
        glu_ref[0, rows] = proj[:, :D_CONV] * jax.nn.sigmoid(proj[:, D_CONV:2 * D_CONV])
        o = 2 * D_CONV
        cq = proj[:, o:o + Q_LORA]
        ckv_raw = proj[:, o + Q_LORA:o + Q_LORA + KV_LORA]
        kp = proj[:, o + Q_LORA + KV_LORA:]
        cqn = (cq * _rms_scale(cq, Q_LORA) * gqa_ref[...]).astype(BF16)
        qq = jnp.dot(cqn, wuq_ref[...], preferred_element_type=F32)
        ckv = ckv_raw * _rms_scale(ckv_raw, KV_LORA) * gkva_ref[...]
        ckv_ref[0, rows] = ckv
        cos = cos_ref[rows]
        sin = sin_ref[rows]
        kpe = kp * cos + pltpu.roll(kp, LANES - QK_ROPE, 1) * sin
        kpe_ref[0, rows] = kpe[:, :QK_ROPE]
        ckv_b = ckv.astype(BF16)
        kn = jnp.dot(ckv_b, wuk_ref[...], preferred_element_type=F32)
        vT = lax.dot_general(wuvT_ref[...], ckv_b, (((1,), (1,)), ((), ())),
                             preferred_element_type=F32).astype(BF16)
        j, c0 = divmod(r0, tq)
        for hd in range(N_HEADS):
            vT_ref[0, j, hd * V_ROWS:hd * V_ROWS + V_DIM, c0:c0 + th] = vT[hd * V_DIM:(hd + 1) * V_DIM]
            vT_ref[0, j, hd * V_ROWS + V_DIM:(hd + 1) * V_ROWS, c0:c0 + th] = ones_rows
        return qq, kn, kpe, cos, sin

    def heads_out(r0, qq, kn, kpe, cos, sin):
        rows = slice(r0, r0 + th)
        cosq = cos + nope
        for hd in range(N_HEADS):
            sl = slice(hd * HEAD_LANES, (hd + 1) * HEAD_LANES)
            qh = qq[:, sl] * cosq + qq[:, D_QK_PAD + hd * HEAD_LANES:D_QK_PAD + (hd + 1) * HEAD_LANES] * sin
            q_ref[0, rows, sl] = (qh * _rms_scale(qh, QK_DIM) * gq).astype(BF16)
            kh = kn[:, sl] + kpe
            k_ref[0, rows, sl] = (kh * _rms_scale(kh, QK_DIM) * gk).astype(BF16)

    starts = range(0, t, th)
    projs = [in_proj(r0) for r0 in starts]
    mids = [low_rank(r0, p) for r0, p in zip(starts, projs)]
    for r0, mid in zip(starts, mids):
        heads_out(r0, *mid)


def _proj(x, cos, sin, w, *, t, tq):
    b, l, _ = x.shape
    th = t // 2 if (t // 2) % LANES == 0 and tq % (t // 2) == 0 else t
    kern = functools.partial(_proj_kernel, tq=tq, th=th)
    out_shape = (
        jax.ShapeDtypeStruct((b, l, D_CONV), F32),
        jax.ShapeDtypeStruct((b, l, D_QK_PAD), BF16),
        jax.ShapeDtypeStruct((b, l, D_QK_PAD), BF16),
        jax.ShapeDtypeStruct((b, l // tq, N_HEADS * V_ROWS, tq), BF16),
        jax.ShapeDtypeStruct((b, l, KV_LORA), F32),
        jax.ShapeDtypeStruct((b, l, QK_ROPE), F32),
    )
    row = lambda bi, i: (bi, i, 0)
    in_specs = [
        pl.BlockSpec((1, t, D_MODEL), row),
        pl.BlockSpec((t, LANES), lambda bi, i: (i, 0)),
        pl.BlockSpec((t, LANES), lambda bi, i: (i, 0)),
        _const_spec((1, D_MODEL)),
        _const_spec((D_MODEL, D_IN_PAD)),
        _const_spec((1, Q_LORA)),
        _const_spec((Q_LORA, 2 * D_QK_PAD)),
        _const_spec((1, KV_LORA)),
        _const_spec((KV_LORA, D_QK_PAD)),
        _const_spec((D_ATTN, KV_LORA)),
        _const_spec((1, LANES)),
        _const_spec((1, LANES)),
    ]
    out_specs = (
        pl.BlockSpec((1, t, D_CONV), row),
        pl.BlockSpec((1, t, D_QK_PAD), row),
        pl.BlockSpec((1, t, D_QK_PAD), row),
        pl.BlockSpec((1, t // tq, N_HEADS * V_ROWS, tq), lambda bi, i: (bi, i, 0, 0)),
        pl.BlockSpec((1, t, KV_LORA), row),
        pl.BlockSpec((1, t, QK_ROPE), row),
    )
    return pl.pallas_call(
        kern, name=f"proj_b{b}", grid=(b, l // t), in_specs=in_specs, out_specs=out_specs,
        out_shape=out_shape,
        compiler_params=_cparams(("parallel", "parallel")),
    )(x, cos, sin, w["gmix"], w["win"], w["gqa"], w["wuq"], w["gkva"], w["wuk"], w["wuvT"],
      w["gq"], w["gk"])


def _softmax_update(m, acc, s, vT):
    m_new = jnp.maximum(m, jnp.max(s, axis=0, keepdims=True))
    p = jnp.exp2(s - m_new).astype(BF16)
    acc = acc * jnp.exp2(m - m_new) + jnp.dot(vT, p, preferred_element_type=F32)
    return m_new, acc


def _attn_kernel(*refs, tq, nq, n_prefix, nh):
    if n_prefix:
        q_ref, k_ref, vT_ref, kp_ref, vTp_ref, o_ref, oT_ref, s0_ref = refs
    else:
        q_ref, k_ref, vT_ref, o_ref, oT_ref, s0_ref = refs
    tb = (((1,), (1,)), ((), ()))
    row_i = lax.broadcasted_iota(jnp.int32, (tq, tq), 0)
    col_i = lax.broadcasted_iota(jnp.int32, (tq, tq), 1)
    causal = row_i <= col_i
    heads = range(nh)
    lanes = [slice(h * HEAD_LANES, (h + 1) * HEAD_LANES) for h in heads]
    vrows = [slice(h * V_ROWS, (h + 1) * V_ROWS) for h in heads]

    def q_body(qi, carry):
        q0 = pl.multiple_of(qi * tq, tq)
        q_t = [q_ref[0, pl.ds(q0, tq), lanes[h]] for h in heads]

        def scores(k0, h):
            return lax.dot_general(k_ref[0, pl.ds(k0, tq), lanes[h]], q_t[h], tb,
                                   preferred_element_type=F32)

        s0_ref[...] = scores(0, 0)
        state = []
        for h in heads:
            if n_prefix:
                s0 = lax.dot_general(kp_ref[0, :, lanes[h]], q_t[h], tb, preferred_element_type=F32)
                m = jnp.max(s0, axis=0, keepdims=True)
                acc = jnp.dot(vTp_ref[0, vrows[h], :], jnp.exp2(s0 - m).astype(BF16),
                              preferred_element_type=F32)
            else:
                m = jnp.full((1, tq), -jnp.inf, F32)
                acc = jnp.zeros((V_ROWS, tq), F32)
            state += [m, acc]

        def k_body(kj, c):
            k0 = pl.multiple_of(kj * tq, tq)
            out = []
            s_next = s0_ref[...]
            for h in heads:
                s = s_next
                if h + 1 < nh:
                    s_next = scores(k0, h + 1)
                else:
                    s0_ref[...] = scores(k0 + tq, 0)
                out += _softmax_update(c[2 * h], c[2 * h + 1], s, vT_ref[0, kj, vrows[h], :])
            return tuple(out)

        state = lax.fori_loop(0, qi, k_body, tuple(state))
        hq = tq // 2

        def diag_scores(h):
            sa = lax.dot_general(k_ref[0, pl.ds(q0, hq), lanes[h]], q_t[h], tb,
                                 preferred_element_type=F32)
            sb = lax.dot_general(k_ref[0, pl.ds(q0 + hq, hq), lanes[h]], q_t[h][hq:], tb,
                                 preferred_element_type=F32)
            return sa, sb

        s0 = s0_ref[...]
        d_next = (s0[:hq], s0[hq:, hq:])
        for h in heads:
            sa, sb = d_next
            if h + 1 < nh:
                d_next = diag_scores(h + 1)
            sa = jnp.where(causal[:hq], sa, -jnp.inf)
            sb = jnp.where(causal[hq:, hq:], sb, -jnp.inf)
            m, acc = _softmax_update(state[2 * h], state[2 * h + 1], sa, vT_ref[0, qi, vrows[h], 0:hq])
            _, acc_hi = _softmax_update(m[:, hq:], acc[:, hq:], sb, vT_ref[0, qi, vrows[h], hq:tq])
            oT_ref[qi, h * V_DIM:(h + 1) * V_DIM, 0:hq] = acc[:V_DIM, :hq] / acc[V_DIM:V_DIM + 1, :hq]
            oT_ref[qi, h * V_DIM:(h + 1) * V_DIM, hq:tq] = acc_hi[:V_DIM] / acc_hi[V_DIM:V_DIM + 1]
        return carry

    lax.fori_loop(0, nq, q_body, 0)
    for qi in range(nq):
        o_ref[0, qi * tq:(qi + 1) * tq, :] = oT_ref[qi].T


def _attention(q, k, vT, k_prefix=None, vT_prefix=None, *, tq):
    b, l, _ = q.shape
    nq = l // tq
    n_prefix = 0 if k_prefix is None else k_prefix.shape[1]
    nh = ATTN_HEADS_PER_STEP
    kern = functools.partial(_attn_kernel, tq=tq, nq=nq, n_prefix=n_prefix, nh=nh)
    in_specs = [
        pl.BlockSpec((1, l, nh * HEAD_LANES), lambda bi, g: (bi, 0, g)),
        pl.BlockSpec((1, l, nh * HEAD_LANES), lambda bi, g: (bi, 0, g)),
        pl.BlockSpec((1, nq, nh * V_ROWS, tq), lambda bi, g: (bi, 0, g, 0)),
    ]
    args = [q, k, vT]
    if n_prefix:
        in_specs += [
            pl.BlockSpec((1, n_prefix, nh * HEAD_LANES), lambda bi, g: (0, 0, g)),
            pl.BlockSpec((1, nh * V_ROWS, n_prefix), lambda bi, g: (0, g, 0)),
        ]
        args += [k_prefix, vT_prefix]
    return pl.pallas_call(
        kern, name=f"attn_b{b}", grid=(b, N_HEADS // nh), in_specs=in_specs,
        out_specs=pl.BlockSpec((1, l, nh * V_DIM), lambda bi, g: (bi, 0, g)),
        out_shape=jax.ShapeDtypeStruct((b, l, D_ATTN), F32),
        scratch_shapes=[pltpu.VMEM((nq, nh * V_DIM, tq), F32), pltpu.VMEM((tq, tq), F32)],
        compiler_params=_cparams(("parallel", "parallel")),
    )(*args)


def _conv_tail(y, lng, lnb, gconv):
    mu = jnp.mean(y, axis=-1, keepdims=True)
    yc = y - mu
    var = jnp.mean(yc * yc, axis=-1, keepdims=True)
    z = yc * lax.rsqrt(var + EPS) * lng + lnb
    z = z * jax.nn.sigmoid(z)
    return (z * _rms_scale(z, D_CONV) * gconv).astype(BF16)


def _convout_kernel(x_ref, glu_ref, prev_ref, hist_ref, ao_ref, cw_ref, cb_ref, lng_ref, lnb_ref,
                    gattn_ref, gconv_ref, wout_ref, o_ref, ext_ref, y_ref, *, rc):
    o_ref[0] = _conv_outproj(x_ref, glu_ref, prev_ref, hist_ref, ao_ref, cw_ref, cb_ref, lng_ref,
                             lnb_ref, gattn_ref, gconv_ref, wout_ref, ext_ref, y_ref, rc)


def _conv_outproj(x_ref, glu_ref, prev_ref, hist_ref, ao_ref, cw_ref, cb_ref, lng_ref, lnb_ref,
                  gattn_ref, gconv_ref, wout_ref, ext_ref, y_ref, rc):
    i = pl.program_id(1)
    t = glu_ref.shape[1]
    ext_ref[0:CONV_HIST_PAD] = jnp.where(i == 0, hist_ref[0], prev_ref[0])
    ext_ref[CONV_HIST_PAD:] = glu_ref[0]
    base = CONV_HIST_PAD - (CONV_WIDTH - 1)
    nt = rc // SUBLANES
    for cc in range(D_CONV // LANES):
        cs = slice(cc * LANES, (cc + 1) * LANES)
        for r0 in range(0, t, rc):
            acc = jnp.zeros((nt, SUBLANES, LANES), F32)
            for res in range(SUBLANES):
                taps = [kk for kk in range(CONV_WIDTH) if (base + kk) % SUBLANES == res]
                shifted = ext_ref[r0 + res:r0 + base + taps[-1] + rc, cs].reshape(-1, SUBLANES, LANES)
                for kk in taps:
                    wk = cw_ref[kk, :, cs][None]
                    off = (base + kk - res) // SUBLANES
                    acc = acc + wk * shifted[off:off + nt]
            y_ref[r0:r0 + rc, cs] = acc.reshape(rc, LANES)
    conv_n = _conv_tail(y_ref[...] + cb_ref[...], lng_ref[...], lnb_ref[...], gconv_ref[...])
    ao = ao_ref[0]
    attn_n = (ao * _rms_scale(ao, D_ATTN) * gattn_ref[...]).astype(BF16)
    upd = jnp.dot(attn_n, wout_ref[0:D_ATTN], preferred_element_type=F32)
    upd = upd + jnp.dot(conv_n, wout_ref[D_ATTN:], preferred_element_type=F32)
    return x_ref[0] + upd


def _convout(x, glu, hist, ao, w, *, t):
    b, l, _ = x.shape
    per_batch_hist = hist.shape[0] > 1
    hb = t // CONV_HIST_PAD
    row = lambda bi, i: (bi, i, 0)
    in_specs = [
        pl.BlockSpec((1, t, D_MODEL), row),
        pl.BlockSpec((1, t, D_CONV), row),
        pl.BlockSpec((1, CONV_HIST_PAD, D_CONV), lambda bi, i: (bi, jnp.maximum(i * hb - 1, 0), 0)),
        pl.BlockSpec((1, CONV_HIST_PAD, D_CONV),
                     (lambda bi, i: (bi, 0, 0)) if per_batch_hist else (lambda bi, i: (0, 0, 0))),
        pl.BlockSpec((1, t, D_ATTN), row),
        _const_spec((CONV_WIDTH, SUBLANES, D_CONV)),
        _const_spec((1, D_CONV)),
        _const_spec((1, D_CONV)),
        _const_spec((1, D_CONV)),
        _const_spec((1, D_ATTN)),
        _const_spec((1, D_CONV)),
        _const_spec((D_MODEL, D_MODEL)),
    ]
    return pl.pallas_call(
        functools.partial(_convout_kernel, rc=min(t, 128)), name=f"convout_b{b}",
        grid=(b, l // t), in_specs=in_specs,
        out_specs=pl.BlockSpec((1, t, D_MODEL), row),
        out_shape=jax.ShapeDtypeStruct((b, l, D_MODEL), F32),
        scratch_shapes=[pltpu.VMEM((t + CONV_HIST_PAD, D_CONV), F32), pltpu.VMEM((t, D_CONV), F32)],
        compiler_params=_cparams(("parallel", "arbitrary")),
    )(x, glu, glu, hist, ao, w["conv_w8"], w["conv_b"], w["ln_g"], w["ln_b"], w["gattn"],
      w["gconv"], w["wout"])


def _ffn_kernel(*refs, ts, has_prev, st0):
    if has_prev:
        x_ref, prev_ref, hist_ref, g_ref, wup_ref, cw_ref, cb_ref, wdn_ref, o_ref, st_ref, act_ref = refs
    else:
        x_ref, hist_ref, g_ref, wup_ref, cw_ref, cb_ref, wdn_ref, o_ref, st_ref, act_ref = refs
    i = pl.program_id(1)
    x = x_ref[0]
    t = x.shape[0]
    hr = hist_ref.shape[1]
    g = g_ref[...]
    if has_prev:
        xe = jnp.concatenate([prev_ref[0], x], axis=0)
    else:
        xe = x
    hx = (xe * _rms_scale(xe, D_MODEL) * g).astype(BF16)
    cw = cw_ref[...]
    cb = cb_ref[...]
    w2 = 2 * FFN_CHUNK
    for c in range(D_FF // FFN_CHUNK):
        cs = slice(c * w2, (c + 1) * w2)
        u = jnp.dot(hx, wup_ref[:, cs], preferred_element_type=F32)
        hist = hist_ref[0, :, cs]
        if has_prev:
            u_ext = jnp.concatenate([jnp.where(i == 0, hist, u[:hr]), u[hr:]], axis=0)
        else:
            u_ext = jnp.concatenate([hist, u], axis=0)
        st_ref[0, :, cs] = u_ext[st0:st0 + 2 * ts]
        uc = (cw[0:1, cs] * u_ext[hr - 2 * ts:hr - 2 * ts + t]
              + cw[1:2, cs] * u_ext[hr - ts:hr - ts + t]
              + cw[2:3, cs] * u_ext[hr:] + cb[:, cs])
        gate = uc[:, :FFN_CHUNK]
        act = gate * jax.nn.sigmoid(gate) * uc[:, FFN_CHUNK:]
        act_ref[:, c * FFN_CHUNK:(c + 1) * FFN_CHUNK] = act.astype(BF16)
    o_ref[0] = x + jnp.dot(act_ref[...], wdn_ref[...], preferred_element_type=F32)


def _ffn(x, hist, w, *, t, ts, has_prev, n_valid=None):
    b, l, _ = x.shape
    hr = hist.shape[1]
    st0 = hr + (t if n_valid is None else n_valid) - 2 * ts
    per_batch_hist = hist.shape[0] > 1
    row = lambda bi, i: (bi, i, 0)
    in_specs = [pl.BlockSpec((1, t, D_MODEL), row)]
    args = [x]
    scratch = [pltpu.VMEM((t, D_FF), BF16)]
    if has_prev:
        hb = t // hr
        in_specs.append(pl.BlockSpec((1, hr, D_MODEL), lambda bi, i: (bi, jnp.maximum(i * hb - 1, 0), 0)))
        args.append(x)
    in_specs += [
        pl.BlockSpec((1, hr, 2 * D_FF),
                     (lambda bi, i: (bi, 0, 0)) if per_batch_hist else (lambda bi, i: (0, 0, 0))),
        _const_spec((1, D_MODEL)),
        _const_spec((D_MODEL, 2 * D_FF)),
        _const_spec((SUBLANES, 2 * D_FF)),
        _const_spec((1, 2 * D_FF)),
        _const_spec((D_FF, D_MODEL)),
    ]
    args += [hist, w["gffn"], w["wup"], w["ffn_w"], w["ffn_b"], w["wdn"]]
    out, st = pl.pallas_call(
        functools.partial(_ffn_kernel, ts=ts, has_prev=has_prev, st0=st0), name=f"ffn_b{b}_ts{ts}",
        grid=(b, l // t), in_specs=in_specs,
        out_specs=(pl.BlockSpec((1, t, D_MODEL), row),
                   pl.BlockSpec((1, 2 * ts, 2 * D_FF), lambda bi, i: (bi, 0, 0))),
        out_shape=(jax.ShapeDtypeStruct((b, l, D_MODEL), F32),
                   jax.ShapeDtypeStruct((b, 2 * ts, 2 * D_FF), F32)),
        scratch_shapes=scratch,
        compiler_params=_cparams(("parallel", "arbitrary")),
    )(*args)
    return out, st


def _mixffn_kernel(x_ref, glu_ref, prev_ref, chist_ref, ao_ref, cw_ref, cb_ref, lng_ref, lnb_ref,
                   gattn_ref, gconv_ref, wout_ref, fhist_ref, g_ref, wup_ref, fw_ref, fb_ref, wdn_ref,
                   o_ref, st_ref, ext_ref, y_ref, act_ref, carry_ref, *, rc):
    x1 = _conv_outproj(x_ref, glu_ref, prev_ref, chist_ref, ao_ref, cw_ref, cb_ref, lng_ref,
                       lnb_ref, gattn_ref, gconv_ref, wout_ref, ext_ref, y_ref, rc)
    t = x1.shape[0]
    hr = fhist_ref.shape[1]

    @pl.when(pl.program_id(1) == 0)
    def _():
        carry_ref[...] = fhist_ref[0]

    hx = (x1 * _rms_scale(x1, D_MODEL) * g_ref[...]).astype(BF16)
    fw = fw_ref[...]
    fb = fb_ref[...]
    w2 = 2 * FFN_CHUNK
    for c in range(D_FF // FFN_CHUNK):
        cs = slice(c * w2, (c + 1) * w2)
        u = jnp.dot(hx, wup_ref[:, cs], preferred_element_type=F32)
        u_ext = jnp.concatenate([carry_ref[:, cs], u], axis=0)
        carry_ref[:, cs] = u_ext[t:]
        st_ref[0, :, cs] = u_ext[hr + t - 2:]
        uc = (fw[0:1, cs] * u_ext[hr - 2:hr - 2 + t] + fw[1:2, cs] * u_ext[hr - 1:hr - 1 + t]
              + fw[2:3, cs] * u_ext[hr:] + fb[:, cs])
        gate = uc[:, :FFN_CHUNK]
        act = gate * jax.nn.sigmoid(gate) * uc[:, FFN_CHUNK:]
        act_ref[:, c * FFN_CHUNK:(c + 1) * FFN_CHUNK] = act.astype(BF16)
    o_ref[0] = x1 + jnp.dot(act_ref[...], wdn_ref[...], preferred_element_type=F32)


def _mixffn(x, glu, conv_hist, ao, ffn_hist, w, *, t):
    b, l, _ = x.shape
    hr = ffn_hist.shape[1]
    hb = t // CONV_HIST_PAD
    row = lambda bi, i: (bi, i, 0)
    shared = lambda bi, i: (0, 0, 0)
    in_specs = [
        pl.BlockSpec((1, t, D_MODEL), row),
        pl.BlockSpec((1, t, D_CONV), row),
        pl.BlockSpec((1, CONV_HIST_PAD, D_CONV), lambda bi, i: (bi, jnp.maximum(i * hb - 1, 0), 0)),
        pl.BlockSpec((1, CONV_HIST_PAD, D_CONV), shared),
        pl.BlockSpec((1, t, D_ATTN), row),
        _const_spec((CONV_WIDTH, SUBLANES, D_CONV)),
        _const_spec((1, D_CONV)),
        _const_spec((1, D_CONV)),
        _const_spec((1, D_CONV)),
        _const_spec((1, D_ATTN)),
        _const_spec((1, D_CONV)),
        _const_spec((D_MODEL, D_MODEL)),
        pl.BlockSpec((1, hr, 2 * D_FF), shared),
        _const_spec((1, D_MODEL)),
        _const_spec((D_MODEL, 2 * D_FF)),
        _const_spec((SUBLANES, 2 * D_FF)),
        _const_spec((1, 2 * D_FF)),
        _const_spec((D_FF, D_MODEL)),
    ]
    assert conv_hist.shape[0] == 1 and ffn_hist.shape[0] == 1
    return pl.pallas_call(
        functools.partial(_mixffn_kernel, rc=min(t, 128)), name=f"mixffn_b{b}",
        grid=(b, l // t), in_specs=in_specs,
        out_specs=(pl.BlockSpec((1, t, D_MODEL), row),
                   pl.BlockSpec((1, 2, 2 * D_FF), lambda bi, i: (bi, 0, 0))),
        out_shape=(jax.ShapeDtypeStruct((b, l, D_MODEL), F32),
                   jax.ShapeDtypeStruct((b, 2, 2 * D_FF), F32)),
        scratch_shapes=[pltpu.VMEM((t + CONV_HIST_PAD, D_CONV), F32), pltpu.VMEM((t, D_CONV), F32),
                        pltpu.VMEM((t, D_FF), BF16), pltpu.VMEM((hr, 2 * D_FF), F32)],
        compiler_params=_cparams(("parallel", "arbitrary")),
    )(x, glu, glu, conv_hist, ao, w["conv_w8"], w["conv_b"], w["ln_g"], w["ln_b"], w["gattn"],
      w["gconv"], w["wout"], ffn_hist, w["gffn"], w["wup"], w["ffn_w"], w["ffn_b"], w["wdn"])


def _qlat_kernel(q_ref, gk_ref, wuk_ref, o_ref):
    gk = gk_ref[...]
    n = q_ref.shape[0]
    for hd in range(N_HEADS):
        sl = slice(hd * HEAD_LANES, (hd + 1) * HEAD_LANES)
        qg = q_ref[:, sl].astype(F32) * gk
        lat = lax.dot_general(qg.astype(BF16), wuk_ref[:, sl], (((1,), (1,)), ((), ())),
                              preferred_element_type=F32)
        lane = lax.broadcasted_iota(jnp.int32, (1, LANES), 1)
        rope = jnp.where(lane < QK_ROPE, qg, 0.0)
        o_ref[hd] = jnp.concatenate([lat, rope], axis=1).astype(BF16)


def _qlat(q, w):
    n = q.shape[0]
    return pl.pallas_call(
        _qlat_kernel, name="sample_qlat",
        out_shape=jax.ShapeDtypeStruct((N_HEADS, n, 2 * LANES), BF16),
    )(q, w["gk"], w["wuk"])


def _sattn_kernel(pt_ref, ckv_hbm, kpe_hbm, newc_ref, newp_ref, aconst_ref, qlat_ref, o_ref,
                  cbuf, pbuf, sem, a_ref, *, group, page, n_new, n_groups, nslot):
    b = pl.program_id(0)
    nb = pl.num_programs(0)
    per_round = nslot // 2
    n_rounds = n_groups // per_round
    nconst = aconst_ref.shape[0]
    rows = a_ref.shape[0] - nconst
    tb = (((1,), (1,)), ((), ()))

    def group_copies(bb, g, slot):
        cps = []
        for j in range(group):
            pg = pt_ref[bb, g * group + j]
            cps.append(pltpu.make_async_copy(
                ckv_hbm.at[0, pg], cbuf.at[slot, pl.ds(j * page, page), :], sem.at[0, slot]))
            cps.append(pltpu.make_async_copy(
                kpe_hbm.at[0, pg], pbuf.at[slot, :, pl.ds(j * page, page)], sem.at[1, slot]))
        return cps

    def round_copies(bb, rd, half):
        return [cp for u in range(per_round)
                for cp in group_copies(bb, rd * per_round + u, half * per_round + u)]

    @pl.when(b == 0)
    def _():
        a_ref[0:nconst] = aconst_ref[...]
        for cp in round_copies(0, 0, 0):
            cp.start()

    a_ref[nconst:] = qlat_ref[0, :, 0:KV_LORA]
    q_rope = qlat_ref[0, :, KV_LORA:KV_LORA + QK_ROPE]

    def expand(ckv_f32):
        half = nconst // 2
        ckv_b = ckv_f32.astype(BF16)
        r0 = lax.dot_general(a_ref[0:half], ckv_b, tb, preferred_element_type=F32)
        r1 = lax.dot_general(a_ref[half:], ckv_b, tb, preferred_element_type=F32)
        return ckv_b, r0, r1

    def chain(expanded, kpe_t, mask=None):
        ckv_b, r0, r1 = expanded
        tk = ckv_b.shape[0]
        half = nconst // 2
        kn = jnp.concatenate([r0, r1[0:half]], axis=0)
        ss = jnp.sum((kn * kn).reshape(N_HEADS, QK_NOPE, tk), axis=1)
        ss = ss + jnp.sum(kpe_t * kpe_t, axis=0, keepdims=True)
        rinv = lax.rsqrt(ss * (1.0 / QK_DIM) + EPS)
        sc = r1[half:] + jnp.dot(q_rope, kpe_t.astype(BF16), preferred_element_type=F32)
        s = sc * jnp.concatenate([rinv] * n_new, axis=0)
        if mask is not None:
            s = jnp.where(mask, s, -jnp.inf)
        m = jnp.max(s, axis=1, keepdims=True)
        p = jnp.exp2(s - m)
        return m, jnp.sum(p, axis=1, keepdims=True), jnp.dot(p.astype(BF16), ckv_b,
                                                             preferred_element_type=F32)

    def merge(state, part):
        m_old, l_old, acc_old = state
        m, l, acc = part
        m_new = jnp.maximum(m_old, m)
        w_old = jnp.exp2(m_old - m_new)
        w = jnp.exp2(m - m_new)
        return m_new, l_old * w_old + l * w, acc_old * w_old + acc * w

    def two_rounds(it, state):
        for half in range(2):
            rd = it * 2 + half
            wrap = rd + 1 >= n_rounds
            bb = jnp.minimum(jnp.where(wrap, b + 1, b), nb - 1)
            for cp in round_copies(bb, jnp.where(wrap, 0, rd + 1), 1 - half):
                cp.start()
            for cp in round_copies(b, rd, half):
                cp.wait()
            base = half * per_round
            nxt = expand(cbuf[base])
            for u in range(per_round):
                cur = nxt
                if u + 1 < per_round:
                    nxt = expand(cbuf[base + u + 1])
                state = merge(state, chain(cur, pbuf[base + u]))
        return state

    state = (jnp.full((rows, 1), -jnp.inf, F32), jnp.zeros((rows, 1), F32),
             jnp.zeros((rows, KV_LORA), F32))
    state = lax.fori_loop(0, n_rounds // 2, two_rounds, state)

    tok = lax.broadcasted_iota(jnp.int32, (rows, page), 1)
    step = lax.broadcasted_iota(jnp.int32, (rows, page), 0) // N_HEADS
    _, l, acc = merge(state, chain(expand(newc_ref[0]), newp_ref[0], mask=tok <= step))
    o_ref[0] = acc / l

    @pl.when(b == nb - 1)
    def _():
        for cp in round_copies(b, 0, 0):
            cp.wait()


def _sample_attention(page_table, cache_ckv, cache_kpe_t, newc, newp_t, aconst, qlat, *, group, nslot):
    nb, n_pages = page_table.shape
    page = cache_ckv.shape[2]
    nrow = qlat.shape[1]
    n_new = nrow // N_HEADS
    nconst = aconst.shape[0]
    n_groups = n_pages // group
    assert n_pages % group == 0 and n_groups % nslot == 0
    in_specs = [
        pl.BlockSpec(memory_space=pl.ANY),
        pl.BlockSpec(memory_space=pl.ANY),
        pl.BlockSpec((1, page, KV_LORA), lambda b, pt: (b, 0, 0)),
        pl.BlockSpec((1, QK_ROPE, page), lambda b, pt: (b, 0, 0)),
        pl.BlockSpec((nconst, KV_LORA), lambda b, pt: (0, 0)),
        pl.BlockSpec((1, nrow, 2 * LANES), lambda b, pt: (b, 0, 0)),
    ]
    grid_spec = pltpu.PrefetchScalarGridSpec(
        num_scalar_prefetch=1, grid=(nb,), in_specs=in_specs,
        out_specs=pl.BlockSpec((1, nrow, KV_LORA), lambda b, pt: (b, 0, 0)),
        scratch_shapes=[
            pltpu.VMEM((nslot, group * page, KV_LORA), F32),
            pltpu.VMEM((nslot, QK_ROPE, group * page), F32),
            pltpu.SemaphoreType.DMA((2, nslot)),
            pltpu.VMEM((nconst + nrow, KV_LORA), BF16),
        ])
    return pl.pallas_call(
        functools.partial(_sattn_kernel, group=group, page=page, n_new=n_new, n_groups=n_groups,
                          nslot=nslot),
        name="sample_attn", grid_spec=grid_spec,
        out_shape=jax.ShapeDtypeStruct((nb, nrow, KV_LORA), F32),
        compiler_params=_cparams(("arbitrary",)),
    )(page_table, cache_ckv, cache_kpe_t, newc, newp_t, aconst, qlat)


def _spost_kernel(x_ref, lat_ref, ext_ref, wuvp_ref, cw_ref, cb_ref, lng_ref, lnb_ref, gattn_ref,
                  gconv_ref, wout_ref, o_ref, y_ref, *, nb):
    n = x_ref.shape[0]
    n_new = n // nb
    ao = jnp.zeros((n, D_ATTN), F32)
    for hd in range(N_HEADS):
        ao = ao + jnp.dot(lat_ref[hd].astype(BF16), wuvp_ref[hd], preferred_element_type=F32)
    attn_n = (ao * _rms_scale(ao, D_ATTN) * gattn_ref[...]).astype(BF16)
    cw = cw_ref[...]
    for s in range(n_new):
        acc = jnp.zeros((nb, D_CONV), F32)
        for kk in range(CONV_WIDTH):
            acc = acc + cw[kk:kk + 1] * ext_ref[s + kk]
        y_ref[s * nb:(s + 1) * nb] = acc
    conv_n = _conv_tail(y_ref[...] + cb_ref[...], lng_ref[...], lnb_ref[...], gconv_ref[...])
    upd = jnp.dot(attn_n, wout_ref[0:D_ATTN], preferred_element_type=F32)
    upd = upd + jnp.dot(conv_n, wout_ref[D_ATTN:], preferred_element_type=F32)
    o_ref[...] = x_ref[...] + upd


def _sample_post(x, lat, ext, w, *, nb):
    n = x.shape[0]
    return pl.pallas_call(
        functools.partial(_spost_kernel, nb=nb), name="sample_post",
        out_shape=jax.ShapeDtypeStruct((n, D_MODEL), F32),
        scratch_shapes=[pltpu.VMEM((n, D_CONV), F32)],
        compiler_params=pltpu.CompilerParams(vmem_limit_bytes=VMEM_LIMIT),
    )(x, lat, ext, w["wuv_pad"], w["conv_w"], w["conv_b"], w["ln_g"], w["ln_b"], w["gattn"],
      w["gconv"], w["wout"])


def _head_layout(g):
    return jnp.concatenate([g[QK_NOPE:], g[:QK_NOPE], jnp.zeros((HEAD_LANES - QK_DIM,), F32)])[None]


def _prep_weights(norm_mix_g, w_in, q_a_norm_g, w_uq, kv_a_norm_g, w_ukv, q_norm_g, k_norm_g,
                  conv_dw_w, conv_dw_b, conv_ln_g, conv_ln_b, out_norm_attn_g, out_norm_conv_g,
                  w_out, norm_ffn_g, w_ffn_up, ffn_dw_w, ffn_dw_b, w_ffn_down):
    half = QK_ROPE // 2
    o = 2 * D_CONV + Q_LORA + KV_LORA
    kp1, kp2 = w_in[:, o:o + half], w_in[:, o + half:o + QK_ROPE]
    win = jnp.concatenate([w_in, -kp2, kp1, jnp.zeros((D_MODEL, LANES - 2 * QK_ROPE), F32)], axis=1)

    uq = w_uq.reshape(Q_LORA, N_HEADS, QK_DIM)
    x1, x2 = uq[..., QK_NOPE:QK_NOPE + half], uq[..., QK_NOPE + half:]
    zq = jnp.zeros((Q_LORA, N_HEADS, HEAD_LANES - QK_DIM), F32)
    q_main = jnp.concatenate([x1, x2, uq[..., :QK_NOPE], zq], axis=-1)
    q_rot = jnp.concatenate([-x2, x1, jnp.zeros((Q_LORA, N_HEADS, HEAD_LANES - QK_ROPE), F32)], axis=-1)
    wuq = jnp.concatenate([q_main.reshape(Q_LORA, D_QK_PAD), q_rot.reshape(Q_LORA, D_QK_PAD)], axis=1)

    ukv = w_ukv.reshape(KV_LORA, N_HEADS, QK_NOPE + V_DIM)
    uk, uv = ukv[..., :QK_NOPE], ukv[..., QK_NOPE:]
    wuk = jnp.concatenate([jnp.zeros((KV_LORA, N_HEADS, QK_ROPE), F32), uk,
                           jnp.zeros((KV_LORA, N_HEADS, HEAD_LANES - QK_DIM), F32)],
                          axis=-1).reshape(KV_LORA, D_QK_PAD)
    wuvT = uv.reshape(KV_LORA, D_ATTN).T
    eye = jnp.eye(N_HEADS, dtype=F32)
    wuv_pad = jnp.einsum("chd,hg->hcgd", uv, eye).reshape(N_HEADS, KV_LORA, D_ATTN)
    aconst = uk.transpose(1, 2, 0).reshape(N_HEADS * QK_NOPE, KV_LORA)

    interleave = _interleave
    scale = QK_DIM ** -0.5 * LOG2E
    return dict(
        gmix=norm_mix_g[None], win=win.astype(BF16), gqa=q_a_norm_g[None], wuq=wuq.astype(BF16),
        gkva=kv_a_norm_g[None], wuk=wuk.astype(BF16), wuvT=wuvT.astype(BF16),
        gq=_head_layout(q_norm_g) * scale, gk=_head_layout(k_norm_g),
        wuv_pad=wuv_pad.astype(BF16), aconst=aconst.astype(BF16),
        conv_w=jnp.pad(conv_dw_w, ((0, CONV_HIST_PAD - CONV_WIDTH), (0, 0))), conv_b=conv_dw_b[None],
        conv_w8=jnp.broadcast_to(conv_dw_w[:, None, :], (CONV_WIDTH, SUBLANES, D_CONV)),
        ln_g=conv_ln_g[None], ln_b=conv_ln_b[None], gattn=out_norm_attn_g[None],
        gconv=out_norm_conv_g[None], wout=w_out.astype(BF16), gffn=norm_ffn_g[None],
        wup=interleave(w_ffn_up.astype(BF16)),
        ffn_w=jnp.pad(interleave(ffn_dw_w), ((0, SUBLANES - FFN_CONV_WIDTH), (0, 0))),
        ffn_b=interleave(ffn_dw_b)[None], wdn=w_ffn_down.astype(BF16),
    )


def _interleave(a):
    parts = []
    for c in range(D_FF // FFN_CHUNK):
        parts.append(a[..., c * FFN_CHUNK:(c + 1) * FFN_CHUNK])
        parts.append(a[..., D_FF + c * FFN_CHUNK:D_FF + (c + 1) * FFN_CHUNK])
    return jnp.concatenate(parts, axis=-1)


def _deinterleave(a):
    w2 = 2 * FFN_CHUNK
    nchunk = D_FF // FFN_CHUNK
    gate = [a[..., c * w2:c * w2 + FFN_CHUNK] for c in range(nchunk)]
    up = [a[..., c * w2 + FFN_CHUNK:(c + 1) * w2] for c in range(nchunk)]
    return jnp.concatenate(gate + up, axis=-1)


def _rope_lanes(pos):
    inv_freq = ROPE_THETA ** (-jnp.arange(0, QK_ROPE, 2, dtype=F32) / QK_ROPE)
    ang = pos.astype(F32)[:, None] * inv_freq[None, :]
    pad = jnp.zeros((pos.shape[0], LANES - QK_ROPE), F32)
    cos = jnp.concatenate([jnp.cos(ang), jnp.cos(ang), pad], axis=1)
    sin = jnp.concatenate([jnp.sin(ang), jnp.sin(ang), pad], axis=1)
    return cos, sin


def _prompt_layer(x, pos, conv_hist, ffn_hist, k_prefix, vT_prefix, w, *, t, tq, tr, tf, n_valid=None):
    cos, sin = _rope_lanes(pos)
    glu, q, k, vT, ckv, kpe = _proj(x, cos, sin, w, t=t, tq=tq)
    ao = _attention(q, k, vT, k_prefix, vT_prefix, tq=tq)
    if n_valid is None and tr == tf:
        y, ffn_state = _mixffn(x, glu, conv_hist, ao, ffn_hist, w, t=tr)
    else:
        x1 = _convout(x, glu, conv_hist, ao, w, t=tr)
        y, ffn_state = _ffn(x1, ffn_hist, w, t=tf, ts=1, has_prev=True, n_valid=n_valid)
    return y, glu, k, vT, ckv, kpe, ffn_state


def kernel(x_prompt, x_sample, cache_ckv, cache_kpe, state_conv, state_ffn, page_table, meta_tokens,
           norm_mix_g, w_in, q_a_norm_g, w_uq, kv_a_norm_g, w_ukv, q_norm_g, k_norm_g, conv_dw_w,
           conv_dw_b, conv_ln_g, conv_ln_b, out_norm_attn_g, out_norm_conv_g, w_out, norm_ffn_g,
           w_ffn_up, ffn_dw_w, ffn_dw_b, w_ffn_down):
    assert norm_mix_g.shape[0] == 1, "single-layer step"
    w = _prep_weights(norm_mix_g[0], w_in[0], q_a_norm_g[0], w_uq[0], kv_a_norm_g[0], w_ukv[0],
                      q_norm_g[0], k_norm_g[0], conv_dw_w[0], conv_dw_b[0], conv_ln_g[0],
                      conv_ln_b[0], out_norm_attn_g[0], out_norm_conv_g[0], w_out[0], norm_ffn_g[0],
                      w_ffn_up[0], ffn_dw_w[0], ffn_dw_b[0], w_ffn_down[0])
    bp, seq, _ = x_prompt.shape
    tm = 256
    tp = 512
    tr = 512
    assert seq % tp == 0

    xm = jnp.pad(meta_tokens.astype(F32), ((0, tm - N_META), (0, 0)))[None]
    zc = jnp.zeros((1, CONV_HIST_PAD, D_CONV), F32)
    zf = jnp.zeros((1, FFN_HIST_PAD, 2 * D_FF), F32)
    _, glu_m, k_m, vT_m, ckv_m, kpe_m, st_m = _prompt_layer(
        xm, jnp.arange(tm), zc, zf, None, None, w, t=tm, tq=tm, tr=tm, tf=tm, n_valid=N_META)

    conv_hist = jnp.pad(glu_m[:, :N_META], ((0, 0), (CONV_HIST_PAD - N_META, 0), (0, 0)))
    ffn_hist = jnp.pad(st_m, ((0, 0), (FFN_HIST_PAD - (FFN_CONV_WIDTH - 1), 0), (0, 0)))
    y_prompt, glu_p, _, _, ckv_p, kpe_p, st_p = _prompt_layer(
        x_prompt, N_META + jnp.arange(seq), conv_hist, ffn_hist,
        k_m[:, :N_META], vT_m[:, 0, :, :N_META], w, t=tp, tq=tp, tr=tr, tf=tp)
    ckv_prompt = jnp.concatenate(
        [jnp.broadcast_to(ckv_m[:, :N_META], (bp, N_META, KV_LORA)), ckv_p], axis=1)[None]
    kpe_prompt = jnp.concatenate(
        [jnp.broadcast_to(kpe_m[:, :N_META], (bp, N_META, QK_ROPE)), kpe_p], axis=1)[None]
    conv_prompt = glu_p[:, seq - (CONV_WIDTH - 1):][None]
    ffn_prompt = _deinterleave(st_p)[None]

    nb, ns, _ = x_sample.shape
    n = nb * ns
    n_hist = FFN_CONV_WIDTH - 1
    assert ns >= n_hist and ns <= SUBLANES
    page = cache_ckv.shape[2]
    past_len = page_table.shape[1] * page
    xs = x_sample.transpose(1, 0, 2).reshape(1, n, D_MODEL)
    cos_s, sin_s = _rope_lanes(past_len + jnp.arange(n) // nb)
    glu_s, q_s, _, _, ckv_s, kpe_s = _proj(xs, cos_s, sin_s, w, t=n, tq=n)

    def to_seq_major(a):
        return a.reshape(ns, nb, a.shape[-1]).transpose(1, 0, 2)

    qlat = _qlat(q_s[0], w)
    qlat = qlat.reshape(N_HEADS, ns, nb, 2 * LANES).transpose(2, 1, 0, 3).reshape(nb, ns * N_HEADS, 2 * LANES)
    ckv_sm, kpe_sm = to_seq_major(ckv_s[0]), to_seq_major(kpe_s[0])
    newc = jnp.pad(ckv_sm, ((0, 0), (0, page - ns), (0, 0)))
    newp_t = jnp.pad(kpe_sm.transpose(0, 2, 1), ((0, 0), (0, 0), (0, page - ns)))
    lat = _sample_attention(page_table, cache_ckv, jnp.swapaxes(cache_kpe, 2, 3), newc, newp_t,
                            w["aconst"], qlat, group=SAMPLE_GROUP, nslot=SAMPLE_SLOTS)
    lat = lat.reshape(nb, ns, N_HEADS, KV_LORA).transpose(2, 1, 0, 3).reshape(N_HEADS, n, KV_LORA)
    ext = jnp.concatenate([state_conv[0].transpose(1, 0, 2), glu_s[0].reshape(ns, nb, D_CONV)], axis=0)
    x1_s = _sample_post(xs[0], lat, ext, w, nb=nb)
    hist_s = _interleave(state_ffn[0]).transpose(1, 0, 2).reshape(1, n_hist * nb, 2 * D_FF)
    y_s, st_s = _ffn(x1_s[None], hist_s, w, t=n, ts=nb, has_prev=False)

    y_sample = to_seq_major(y_s[0])
    conv_sample = jnp.concatenate([state_conv[0][:, ns:], to_seq_major(glu_s[0])], axis=1)[None]
    ffn_sample = _deinterleave(st_s[0]).reshape(n_hist, nb, 2 * D_FF).transpose(1, 0, 2)[None]
    return (y_prompt, y_sample, ckv_prompt, kpe_prompt, conv_prompt, ffn_prompt,
            ckv_sm[None], kpe_sm[None], conv_sample, ffn_sample)
```
